```python
import jax, jax.numpy as jnp
from jax import lax
import numpy as np

D_MODEL = 1024
BATCH = 2
SEQ = 8192
DEPTH = 2
DEC_BATCH = 32
DEC_SEQ = 1
PAST_LEN = 16384
PAGE_SIZE = 128

N_MIXERS = 2
N_GMLP_LAYERS = (DEPTH + 1) // 2
N_MOBA_LAYERS = DEPTH // 2
D_FF = 2816
CHUNK = 128
GMLP_WIDTH = D_MODEL
GMLP_GROUPS = 8
GMLP_GROUP_DIM = GMLP_WIDTH // GMLP_GROUPS
N_HEADS = 16
HEAD_DIM = D_MODEL // N_HEADS
MOBA_BLOCK = 256
MOBA_TOPK = 3
Q_BLOCK = 16
EPS = 1e-6
NEG_INF = -1e30

kernel_name = "moba_gmlp_macaron_step"


def _rmsnorm(x, g):
    xf = x.astype(jnp.float32)
    y = xf * lax.rsqrt(jnp.mean(xf * xf, axis=-1, keepdims=True) + EPS)
    return (y * g.astype(jnp.float32)).astype(x.dtype)


def _swiglu(x, w_gate, w_up, w_down):
    return (jax.nn.silu(x @ w_gate) * (x @ w_up)) @ w_down


def _gmlp_mixer(h, w_in, vn_g, vn_b, w_s, b_s, w_out):
    B, T, _ = h.shape
    z = jax.nn.gelu(h @ w_in)
    u, v = jnp.split(z, 2, axis=-1)
    u = u.reshape(B, T, GMLP_GROUPS, GMLP_GROUP_DIM)
    vf = v.reshape(B, T, GMLP_GROUPS, GMLP_GROUP_DIM).astype(jnp.float32)
    mu = jnp.mean(vf, axis=-1, keepdims=True)
    var = jnp.mean(jnp.square(vf - mu), axis=-1, keepdims=True)
    v = ((vf - mu) * lax.rsqrt(var + EPS) * vn_g + vn_b).astype(h.dtype)
    n_chunks = -(-T // CHUNK)
    vc = jnp.pad(v, ((0, 0), (0, n_chunks * CHUNK - T), (0, 0), (0, 0)))
    vc = vc.reshape(B, n_chunks, CHUNK, GMLP_GROUPS, GMLP_GROUP_DIM)
    causal = jnp.tril(jnp.ones((CHUNK, CHUNK), dtype=bool))
    ws = jnp.where(causal, w_s, jnp.zeros_like(w_s))
    mixed = jnp.einsum('gnm,bcmgd->bcngd', ws, vc) + jnp.transpose(b_s)[:, :, None]
    mixed = mixed.reshape(B, n_chunks * CHUNK, GMLP_GROUPS, GMLP_GROUP_DIM)[:, :T]
    out = (u * mixed).reshape(B, T, GMLP_WIDTH) @ w_out
    return out, v


def _qkv(h, w):
    B, T, _ = h.shape
    qkv = (h @ w).reshape(B, T, 3, N_HEADS, HEAD_DIM)
    return qkv[:, :, 0], qkv[:, :, 1], qkv[:, :, 2]


def _moba_select(q, block_means, own_block):
    s = jnp.einsum('bthd,bnhd->bthn', q.astype(jnp.float32), block_means.astype(jnp.float32))
    nb = block_means.shape[1]
    past = jnp.arange(nb)[None, None, None, :] < own_block[None, :, None, None]
    s = jnp.where(past, s, NEG_INF)
    _, idx = lax.top_k(s, min(MOBA_TOPK, nb))
    valid = idx < own_block[None, :, None, None]
    return idx, valid


def _moba_attend(q, k_g, v_g, mask):
    logits = jnp.einsum('bthd,bthld->bthl', q.astype(jnp.float32), k_g.astype(jnp.float32)) * (HEAD_DIM ** -0.5)
    logits = jnp.where(mask, logits, NEG_INF)
    p = jax.nn.softmax(logits, axis=-1)
    return jnp.einsum('bthl,bthld->bthd', p.astype(v_g.dtype), v_g)


def _moba_prompt(q, k, v):
    B, S, H, Dh = q.shape
    nb = -(-S // MOBA_BLOCK)
    pad = nb * MOBA_BLOCK - S
    kb = jnp.pad(k, ((0, 0), (0, pad), (0, 0), (0, 0))).reshape(B, nb, MOBA_BLOCK, H, Dh)
    vb = jnp.pad(v, ((0, 0), (0, pad), (0, 0), (0, 0))).reshape(B, nb, MOBA_BLOCK, H, Dh)
    means = jnp.mean(kb, axis=2, dtype=jnp.float32)
    bidx = jnp.arange(B)[:, None, None, None]
    hidx = jnp.arange(H)[None, None, :, None]
    offs = jnp.arange(MOBA_BLOCK)

    def one_block(c):
        start = c * Q_BLOCK
        qc = lax.dynamic_slice_in_dim(q, start, Q_BLOCK, axis=1)
        pos = start + jnp.arange(Q_BLOCK)
        own = pos // MOBA_BLOCK
        idx, valid = _moba_select(qc, means, own)
        kk = idx.shape[-1]
        k_sel = kb[bidx, idx, :, hidx].reshape(B, Q_BLOCK, H, kk * MOBA_BLOCK, Dh)
        v_sel = vb[bidx, idx, :, hidx].reshape(B, Q_BLOCK, H, kk * MOBA_BLOCK, Dh)
        ob = start // MOBA_BLOCK
        k_own = jnp.transpose(lax.dynamic_index_in_dim(kb, ob, axis=1, keepdims=False), (0, 2, 1, 3))[:, None]
        v_own = jnp.transpose(lax.dynamic_index_in_dim(vb, ob, axis=1, keepdims=False), (0, 2, 1, 3))[:, None]
        k_own = jnp.broadcast_to(k_own, (B, Q_BLOCK, H, MOBA_BLOCK, Dh))
        v_own = jnp.broadcast_to(v_own, (B, Q_BLOCK, H, MOBA_BLOCK, Dh))
        sel_mask = jnp.broadcast_to(valid[..., None], (B, Q_BLOCK, H, kk, MOBA_BLOCK)).reshape(B, Q_BLOCK, H, kk * MOBA_BLOCK)
        own_mask = (ob * MOBA_BLOCK + offs)[None, :] <= pos[:, None]
        own_mask = jnp.broadcast_to(own_mask[None, :, None, :], (B, Q_BLOCK, H, MOBA_BLOCK))
        k_g = jnp.concatenate([k_sel, k_own], axis=3)
        v_g = jnp.concatenate([v_sel, v_own], axis=3)
        mask = jnp.concatenate([sel_mask, own_mask], axis=3)
        return _moba_attend(qc, k_g, v_g, mask)

    outs = lax.map(one_block, jnp.arange(S // Q_BLOCK))
    return jnp.moveaxis(outs, 0, 1).reshape(B, S, H * Dh)


def _moba_sample(q, k_new, v_new, cache_k, cache_v, page_table):
    B, T, H, Dh = q.shape
    n_pages = page_table.shape[1]
    past_len = n_pages * PAGE_SIZE
    nb = -(-(past_len + T) // MOBA_BLOCK)
    pos = past_len + jnp.arange(T)
    own = pos // MOBA_BLOCK
    page_sums = jnp.sum(cache_k, axis=1, dtype=jnp.float32)[page_table]
    page_block = (jnp.arange(n_pages) * PAGE_SIZE) // MOBA_BLOCK
    block_sums = jax.ops.segment_sum(jnp.moveaxis(page_sums, 1, 0), page_block, num_segments=nb)
    block_sums = block_sums + jax.ops.segment_sum(jnp.moveaxis(k_new.astype(jnp.float32), 1, 0), own, num_segments=nb)
    means = jnp.moveaxis(block_sums, 0, 1) / MOBA_BLOCK
    idx, valid = _moba_select(q, means, own)
    kk = idx.shape[-1]
    offs = jnp.arange(MOBA_BLOCK)
    sel_pos = (idx[..., None] * MOBA_BLOCK + offs).reshape(B, T, H, kk * MOBA_BLOCK)
    own_pos = jnp.broadcast_to((own[:, None] * MOBA_BLOCK + offs)[None, :, None, :], (B, T, H, MOBA_BLOCK))
    key_pos = jnp.concatenate([sel_pos, own_pos], axis=-1)
    sel_mask = jnp.broadcast_to(valid[..., None], (B, T, H, kk, MOBA_BLOCK)).reshape(B, T, H, kk * MOBA_BLOCK)
    own_mask = own_pos <= pos[None, :, None, None]
    mask = jnp.concatenate([sel_mask, own_mask], axis=-1)
    bidx = jnp.arange(B)[:, None, None, None]
    hidx = jnp.arange(H)[None, None, :, None]
    in_past = key_pos < past_len
    p_past = jnp.clip(key_pos, 0, past_len - 1)
    phys = page_table[bidx, p_past // PAGE_SIZE]
    off = p_past % PAGE_SIZE
    p_new = jnp.clip(key_pos - past_len, 0, T - 1)
    k_g = jnp.where(in_past[..., None], cache_k[phys, off, hidx], k_new[bidx, p_new, hidx])
    v_g = jnp.where(in_past[..., None], cache_v[phys, off, hidx], v_new[bidx, p_new, hidx])
    return _moba_attend(q, k_g, v_g, mask).reshape(B, T, H * Dh)


def setup_inputs(seed: int = 0) -> dict:
    key = jax.random.key(seed)
    ks = jax.random.split(key, 26)

    def nrm(k, shape, scale):
        return jax.random.normal(k, shape, jnp.float32) * scale

    n_pages = PAST_LEN // PAGE_SIZE
    n_used = DEC_BATCH * n_pages
    n_pool = n_used + n_used // 4
    perm = jax.random.permutation(ks[0], n_pool)
    page_table = perm[:n_used].reshape(DEC_BATCH, n_pages).astype(jnp.int32)
    return {
        'x_prompt': nrm(ks[1], (BATCH, SEQ, D_MODEL), 1.0),
        'x_sample': nrm(ks[2], (DEC_BATCH, DEC_SEQ, D_MODEL), 1.0),
        'cache_k': nrm(ks[3], (N_MOBA_LAYERS, n_pool, PAGE_SIZE, N_HEADS, HEAD_DIM), 1.0),
        'cache_v': nrm(ks[4], (N_MOBA_LAYERS, n_pool, PAGE_SIZE, N_HEADS, HEAD_DIM), 1.0),
        'page_table': page_table,
        'norm_ffn1': 1.0 + nrm(ks[5], (DEPTH, D_MODEL), 0.1),
        'ffn1_gate': nrm(ks[6], (DEPTH, D_MODEL, D_FF), D_MODEL ** -0.5),
        'ffn1_up': nrm(ks[7], (DEPTH, D_MODEL, D_FF), D_MODEL ** -0.5),
        'ffn1_down': nrm(ks[8], (DEPTH, D_FF, D_MODEL), D_FF ** -0.5),
        'norm_mix': 1.0 + nrm(ks[9], (DEPTH, D_MODEL), 0.1),
        'norm_ffn2': 1.0 + nrm(ks[10], (DEPTH, D_MODEL), 0.1),
        'ffn2_gate': nrm(ks[11], (DEPTH, D_MODEL, D_FF), D_MODEL ** -0.5),
        'ffn2_up': nrm(ks[12], (DEPTH, D_MODEL, D_FF), D_MODEL ** -0.5),
        'ffn2_down': nrm(ks[13], (DEPTH, D_FF, D_MODEL), D_FF ** -0.5),
        'gmlp_in': nrm(ks[14], (N_GMLP_LAYERS, D_MODEL, 2 * GMLP_WIDTH), D_MODEL ** -0.5),
        'gmlp_vnorm_g': 1.0 + nrm(ks[15], (N_GMLP_LAYERS, GMLP_GROUPS, GMLP_GROUP_DIM), 0.1),
        'gmlp_vnorm_b': nrm(ks[16], (N_GMLP_LAYERS, GMLP_GROUPS, GMLP_GROUP_DIM), 0.1),
        'gmlp_ws': nrm(ks[17], (N_GMLP_LAYERS, GMLP_GROUPS, CHUNK, CHUNK), CHUNK ** -0.5),
        'gmlp_bs': 1.0 + nrm(ks[18], (N_GMLP_LAYERS, GMLP_GROUPS, CHUNK), 0.1),
        'gmlp_out': nrm(ks[19], (N_GMLP_LAYERS, GMLP_WIDTH, D_MODEL), GMLP_WIDTH ** -0.5),
        'moba_qkv': nrm(ks[20], (N_MOBA_LAYERS, D_MODEL, 3 * N_HEADS * HEAD_DIM), D_MODEL ** -0.5),
        'moba_out': nrm(ks[21], (N_MOBA_LAYERS, N_HEADS * HEAD_DIM, D_MODEL), (N_HEADS * HEAD_DIM) ** -0.5),
        'final_norm': 1.0 + nrm(ks[22], (D_MODEL,), 0.1),
    }


def reference(x_prompt, x_sample, cache_k, cache_v, page_table,
              norm_ffn1, ffn1_gate, ffn1_up, ffn1_down, norm_mix,
              norm_ffn2, ffn2_gate, ffn2_up, ffn2_down,
              gmlp_in, gmlp_vnorm_g, gmlp_vnorm_b, gmlp_ws, gmlp_bs, gmlp_out,
              moba_qkv, moba_out, final_norm):
    yp, ys = x_prompt, x_sample
    Bp, S, _ = x_prompt.shape
    Bs, T, _ = x_sample.shape
    sv_p, sv_s, kp_l, vp_l, ks_l, vs_l = [], [], [], [], [], []
    for i in range(DEPTH):
        yp = yp + 0.5 * _swiglu(_rmsnorm(yp, norm_ffn1[i]), ffn1_gate[i], ffn1_up[i], ffn1_down[i])
        ys = ys + 0.5 * _swiglu(_rmsnorm(ys, norm_ffn1[i]), ffn1_gate[i], ffn1_up[i], ffn1_down[i])
        j = i // N_MIXERS
        hp = _rmsnorm(yp, norm_mix[i])
        hs = _rmsnorm(ys, norm_mix[i])
        if i % N_MIXERS == 0:
            out_p, v_p = _gmlp_mixer(hp, gmlp_in[j], gmlp_vnorm_g[j], gmlp_vnorm_b[j], gmlp_ws[j], gmlp_bs[j], gmlp_out[j])
            out_s, v_s = _gmlp_mixer(hs, gmlp_in[j], gmlp_vnorm_g[j], gmlp_vnorm_b[j], gmlp_ws[j], gmlp_bs[j], gmlp_out[j])
            sv_p.append(v_p[:, -CHUNK:])
            sv_s.append(v_s)
        else:
            qp, kp, vp = _qkv(hp, moba_qkv[j])
            qs, kn, vn = _qkv(hs, moba_qkv[j])
            out_p = _moba_prompt(qp, kp, vp) @ moba_out[j]
            out_s = _moba_sample(qs, kn, vn, cache_k[j], cache_v[j], page_table) @ moba_out[j]
            kp_l.append(kp)
            vp_l.append(vp)
            ks_l.append(kn)
            vs_l.append(vn)
        yp = yp + out_p
        ys = ys + out_s
        yp = yp + 0.5 * _swiglu(_rmsnorm(yp, norm_ffn2[i]), ffn2_gate[i], ffn2_up[i], ffn2_down[i])
        ys = ys + 0.5 * _swiglu(_rmsnorm(ys, norm_ffn2[i]), ffn2_gate[i], ffn2_up[i], ffn2_down[i])
    y_prompt = _rmsnorm(yp, final_norm)
    y_sample = _rmsnorm(ys, final_norm)
    return (y_prompt, y_sample, jnp.stack(sv_p), jnp.stack(sv_s), jnp.stack(kp_l), jnp.stack(vp_l), jnp.stack(ks_l), jnp.stack(vs_l))
```

```python
import functools

import jax
import jax.numpy as jnp
from jax import lax
from jax.experimental import pallas as pl
from jax.experimental.pallas import tpu as pltpu

F32 = jnp.float32
BF16 = jnp.bfloat16

EPS = 1e-6
NEG_INF = -1e30
BELOW_NEG_INF = -3e38

CHUNK = 128
GMLP_GROUPS = 8
N_HEADS = 16
HEAD_DIM = 64
MOBA_BLOCK = 256
MOBA_TOPK = 3
PAGE_SIZE = 128
PAGES_PER_BLOCK = MOBA_BLOCK // PAGE_SIZE

LANES = 128
HEADS_PER_TILE = LANES // HEAD_DIM
VMEM_LIMIT_BYTES = 56 * 1024 * 1024


def _params(n_axes):
    return pltpu.CompilerParams(dimension_semantics=("arbitrary",) * n_axes,
                                vmem_limit_bytes=VMEM_LIMIT_BYTES)


def _rms(x, g):
    return x * lax.rsqrt(jnp.mean(x * x, axis=-1, keepdims=True) + EPS) * g


def _gelu_tanh(x):
    return 0.5 * x * (1.0 + jnp.tanh(0.7978845608028654 * (x + 0.044715 * (x * x * x))))


def _dot(a, b):
    return jnp.dot(a, b, preferred_element_type=F32)


def _dot_nt(a, b):
    return lax.dot_general(a, b, (((1,), (1,)), ((), ())), preferred_element_type=F32)


def _ffn_kernel(x_ref, g_ref, wg_ref, wu_ref, wd_ref, *rest, final_norm):
    if final_norm:
        fg_ref, o_ref, h_ref, acc_ref = rest
    else:
        o_ref, h_ref, acc_ref = rest
    j = pl.program_id(1)

    @pl.when(j == 0)
    def _():
        h_ref[...] = _rms(x_ref[...], g_ref[...]).astype(BF16)
        acc_ref[...] = jnp.zeros_like(acc_ref)

    h = h_ref[...]
    gate = _dot(h, wg_ref[...])
    up = _dot(h, wu_ref[...])
    act = (gate * jax.nn.sigmoid(gate) * up).astype(BF16)
    acc_ref[...] += _dot(act, wd_ref[...])

    @pl.when(j == pl.num_programs(1) - 1)
    def _():
        y = x_ref[...] + 0.5 * acc_ref[...]
        if final_norm:
            y = _rms(y, fg_ref[...])
        o_ref[...] = y


def _ffn(x, g, wg, wu, wd, final_g=None, *, tm, tf=256):
    t, d = x.shape
    ff = wg.shape[1]
    in_specs = [
        pl.BlockSpec((tm, d), lambda i, j: (i, 0)),
        pl.BlockSpec((1, d), lambda i, j: (0, 0)),
        pl.BlockSpec((d, tf), lambda i, j: (0, j)),
        pl.BlockSpec((d, tf), lambda i, j: (0, j)),
        pl.BlockSpec((tf, d), lambda i, j: (j, 0)),
    ]
    args = [x, g.reshape(1, d), wg, wu, wd]
    if final_g is not None:
        in_specs.append(pl.BlockSpec((1, d), lambda i, j: (0, 0)))
        args.append(final_g.reshape(1, d))
    return pl.pallas_call(
        functools.partial(_ffn_kernel, final_norm=final_g is not None),
        grid=(t // tm, ff // tf),
        in_specs=in_specs,
        out_specs=pl.BlockSpec((tm, d), lambda i, j: (i, 0)),
        out_shape=jax.ShapeDtypeStruct((t, d), F32),
        scratch_shapes=[pltpu.VMEM((tm, d), BF16), pltpu.VMEM((tm, d), F32)],
        compiler_params=_params(2),
        name="ffn",
    )(*args)


def _group_layernorm(v, gain, bias):
    mu = jnp.mean(v, axis=-1, keepdims=True)
    d = v - mu
    var = jnp.mean(d * d, axis=-1, keepdims=True)
    return d * lax.rsqrt(var + EPS) * gain + bias


def _gmlp_prompt_kernel(x_ref, g_ref, win_ref, vg_ref, vb_ref, ws_ref, bs_ref, wout_ref,
                        o_ref, sv_ref):
    tm, d = x_ref.shape
    width = win_ref.shape[1] // 2
    gd = width // GMLP_GROUPS
    n_chunks = tm // CHUNK
    x = x_ref[...]
    h = _rms(x, g_ref[...]).astype(BF16)
    z = _gelu_tanh(_dot(h, win_ref[...]))
    row = lax.broadcasted_iota(jnp.int32, (CHUNK, CHUNK), 0)
    col = lax.broadcasted_iota(jnp.int32, (CHUNK, CHUNK), 1)
    is_last = pl.program_id(1) == pl.num_programs(1) - 1
    gated = []
    for grp in range(GMLP_GROUPS):
        lanes = slice(grp * gd, (grp + 1) * gd)
        vn = _group_layernorm(z[:, width + grp * gd: width + (grp + 1) * gd],
                              vg_ref[:, lanes], vb_ref[:, lanes])

        @pl.when(is_last)
        def _(vn=vn, lanes=lanes):
            sv_ref[0, :, lanes] = vn[tm - CHUNK:, :]

        ws = jnp.where(row >= col, ws_ref[grp], 0.0).astype(BF16)
        rhs = jnp.concatenate([vn[c * CHUNK:(c + 1) * CHUNK] for c in range(n_chunks)],
                              axis=1).astype(BF16)
        mixed_cat = _dot(ws, rhs)
        mixed = jnp.concatenate(
            [mixed_cat[:, c * gd:(c + 1) * gd] + bs_ref[grp] for c in range(n_chunks)], axis=0)
        gated.append((z[:, lanes] * mixed).astype(BF16))
    a = jnp.concatenate(gated, axis=1)
    o_ref[...] = x + _dot(a, wout_ref[...])


def _gmlp_prompt(x, g, w_in, vn_g, vn_b, w_s, b_s, w_out, *, batch, tm):
    t, d = x.shape
    width = w_out.shape[0]
    gd = width // GMLP_GROUPS
    tiles = t // batch // tm
    bs_b = jnp.broadcast_to(b_s[:, :, None], (GMLP_GROUPS, CHUNK, gd))
    const2 = lambda b, i: (0, 0)
    const3 = lambda b, i: (0, 0, 0)
    return pl.pallas_call(
        _gmlp_prompt_kernel,
        grid=(batch, tiles),
        in_specs=[
            pl.BlockSpec((tm, d), lambda b, i: (b * tiles + i, 0)),
            pl.BlockSpec((1, d), const2),
            pl.BlockSpec((d, 2 * width), const2),
            pl.BlockSpec((1, width), const2),
            pl.BlockSpec((1, width), const2),
            pl.BlockSpec((GMLP_GROUPS, CHUNK, CHUNK), const3),
            pl.BlockSpec((GMLP_GROUPS, CHUNK, gd), const3),
            pl.BlockSpec((width, d), const2),
        ],
        out_specs=[
            pl.BlockSpec((tm, d), lambda b, i: (b * tiles + i, 0)),
            pl.BlockSpec((1, CHUNK, width), lambda b, i: (b, 0, 0)),
        ],
        out_shape=[jax.ShapeDtypeStruct((t, d), F32),
                   jax.ShapeDtypeStruct((batch, CHUNK, width), F32)],
        compiler_params=_params(2),
        name="gmlp_prompt",
    )(x, g.reshape(1, d), w_in, vn_g.reshape(1, width), vn_b.reshape(1, width), w_s, bs_b, w_out)


def _gmlp_sample_kernel(x_ref, g_ref, win_ref, vg_ref, vb_ref, coef_ref, bias_ref, wout_ref,
                        o_ref, sv_ref):
    width = win_ref.shape[1] // 2
    gd = width // GMLP_GROUPS
    x = x_ref[...]
    h = _rms(x, g_ref[...]).astype(BF16)
    z = _gelu_tanh(_dot(h, win_ref[...]))
    vn = jnp.concatenate(
        [_group_layernorm(z[:, width + grp * gd: width + (grp + 1) * gd],
                          vg_ref[:, grp * gd:(grp + 1) * gd], vb_ref[:, grp * gd:(grp + 1) * gd])
         for grp in range(GMLP_GROUPS)], axis=1)
    sv_ref[...] = vn
    mixed = vn * coef_ref[...] + bias_ref[...]
    o_ref[...] = x + _dot((z[:, :width] * mixed).astype(BF16), wout_ref[...])


def _gmlp_sample(x, g, w_in, vn_g, vn_b, w_s, b_s, w_out):
    t, d = x.shape
    width = w_out.shape[0]
    gd = width // GMLP_GROUPS
    coef = jnp.repeat(w_s[:, 0, 0], gd).reshape(1, width)
    bias = jnp.repeat(b_s[:, 0], gd).reshape(1, width)
    return pl.pallas_call(
        _gmlp_sample_kernel,
        out_shape=[jax.ShapeDtypeStruct((t, d), F32), jax.ShapeDtypeStruct((t, width), F32)],
        compiler_params=pltpu.CompilerParams(vmem_limit_bytes=VMEM_LIMIT_BYTES),
        name="gmlp_sample",
    )(x, g.reshape(1, d), w_in, vn_g.reshape(1, width), vn_b.reshape(1, width), coef, bias, w_out)


def _qkv_prompt_kernel(x_ref, g_ref, w_ref, kf_ref, vf_ref, qb_ref, kb_ref, vb_ref, mean_ref):
    d = x_ref.shape[1]
    h = _rms(x_ref[...], g_ref[...]).astype(BF16)
    qkv = _dot(h, w_ref[...])
    q, k, v = qkv[:, :d], qkv[:, d:2 * d], qkv[:, 2 * d:]
    kf_ref[...] = k
    vf_ref[...] = v
    qb_ref[...] = (q * (HEAD_DIM ** -0.5)).astype(BF16)
    kb_ref[...] = k.astype(BF16)
    vb_ref[...] = v.astype(BF16)
    mean_ref[0] = jnp.sum(k, axis=0, keepdims=True) * (1.0 / MOBA_BLOCK)


def _qkv_prompt(x, g, w):
    t, d = x.shape
    tm = MOBA_BLOCK
    row_spec = pl.BlockSpec((tm, d), lambda i: (i, 0))
    return pl.pallas_call(
        _qkv_prompt_kernel,
        grid=(t // tm,),
        in_specs=[row_spec, pl.BlockSpec((1, d), lambda i: (0, 0)),
                  pl.BlockSpec((d, 3 * d), lambda i: (0, 0))],
        out_specs=[row_spec] * 5 + [pl.BlockSpec((1, 1, d), lambda i: (i, 0, 0))],
        out_shape=[jax.ShapeDtypeStruct((t, d), F32)] * 2 + [jax.ShapeDtypeStruct((t, d), BF16)] * 3
        + [jax.ShapeDtypeStruct((t // tm, 1, d), F32)],
        compiler_params=_params(1),
        name="qkv_prompt",
    )(x, g.reshape(1, d), w)


def _qkv_sample_kernel(x_ref, g_ref, w_ref, q_ref, k_ref, v_ref):
    d = x_ref.shape[1]
    h = _rms(x_ref[...], g_ref[...]).astype(BF16)
    qkv = _dot(h, w_ref[...])
    q_ref[...] = qkv[:, :d]
    k_ref[...] = qkv[:, d:2 * d]
    v_ref[...] = qkv[:, 2 * d:]


def _qkv_sample(x, g, w):
    t, d = x.shape
    return pl.pallas_call(
        _qkv_sample_kernel,
        out_shape=[jax.ShapeDtypeStruct((t, d), F32)] * 3,
        compiler_params=pltpu.CompilerParams(vmem_limit_bytes=VMEM_LIMIT_BYTES),
        name="qkv_sample",
    )(x, g.reshape(1, d), w)


def _proj_kernel(x_ref, a_ref, w_ref, o_ref):
    o_ref[...] = x_ref[...] + _dot(a_ref[...].astype(BF16), w_ref[...])


def _proj_residual(x, a, w, *, tm):
    t, d = x.shape
    row_spec = pl.BlockSpec((tm, d), lambda i: (i, 0))
    return pl.pallas_call(
        _proj_kernel,
        grid=(t // tm,),
        in_specs=[row_spec, row_spec, pl.BlockSpec((d, d), lambda i: (0, 0))],
        out_specs=row_spec,
        out_shape=jax.ShapeDtypeStruct((t, d), F32),
        compiler_params=_params(1),
        name="proj_residual",
    )(x, a, w)


def _top_k_mask(scores, n_iota, n_valid):
    axis = scores.ndim - 2
    n_total = scores.shape[axis]
    s = jnp.where(n_iota < n_valid, scores, NEG_INF)
    picked = jnp.zeros(scores.shape, F32)
    idxs = []
    for _ in range(MOBA_TOPK):
        best = jnp.max(s, axis=axis, keepdims=True)
        idx = jnp.min(jnp.where(s == best, n_iota, n_total), axis=axis, keepdims=True)
        hit = n_iota == idx
        picked = jnp.where(hit, 1.0, picked)
        s = jnp.where(hit, BELOW_NEG_INF, s)
        idxs.append(idx)
    return picked, idxs


def _select_prompt_kernel(q_ref, mt_ref, o_ref):
    tq = q_ref.shape[0]
    nb = mt_ref.shape[1] // N_HEADS
    own = pl.program_id(1)
    st = _dot_nt(mt_ref[0], q_ref[...])
    st = st.reshape(N_HEADS, nb, tq)
    n_iota = lax.broadcasted_iota(jnp.int32, st.shape, 1)
    picked, _ = _top_k_mask(st, n_iota, own)
    bias = jnp.where(n_iota < own, jnp.where(picked > 0.0, 0.0, NEG_INF), NEG_INF)
    pad = jnp.zeros((LANES - HEADS_PER_TILE * nb, tq), F32)
    rows = []
    for hp in range(N_HEADS // HEADS_PER_TILE):
        rows += [bias[HEADS_PER_TILE * hp + hh] for hh in range(HEADS_PER_TILE)] + [pad]
    o_ref[...] = jnp.concatenate(rows, axis=0).T.astype(BF16)


def _select_prompt(qb, means_t, *, batch):
    t, d = qb.shape
    tq = MOBA_BLOCK
    tiles = t // batch // tq
    n_pairs = N_HEADS // HEADS_PER_TILE
    return pl.pallas_call(
        _select_prompt_kernel,
        grid=(batch, tiles),
        in_specs=[pl.BlockSpec((tq, d), lambda b, i: (b * tiles + i, 0)),
                  pl.BlockSpec((1, means_t.shape[1], d), lambda b, i: (b, 0, 0))],
        out_specs=pl.BlockSpec((tq, n_pairs * LANES), lambda b, i: (b * tiles + i, 0)),
        out_shape=jax.ShapeDtypeStruct((t, n_pairs * LANES), BF16),
        compiler_params=_params(2),
        name="select_prompt",
    )(qb, means_t)


def _attn_prompt_kernel(q_ref, b_ref, k_ref, v_ref, o_ref):
    tq = q_ref.shape[0]
    qi = pl.program_id(2)
    q = q_ref[...]
    bias = b_ref[...]
    lane = lax.broadcasted_iota(jnp.int32, (tq, LANES), 1)
    nb = k_ref.shape[0] // MOBA_BLOCK
    zero = jnp.zeros_like(q)
    qs = jnp.concatenate([jnp.where(lane < HEAD_DIM, q, zero), jnp.where(lane < HEAD_DIM, zero, q)],
                         axis=0)
    bs = jnp.concatenate([jnp.where(lane < nb, bias, zero),
                          jnp.where((lane >= nb) & (lane < 2 * nb), bias, zero)], axis=0)
    qa = jnp.concatenate([qs, bs], axis=1)

    start = pl.multiple_of(qi * tq, tq)
    s = _dot_nt(qs, k_ref[pl.ds(start, tq), :])
    r_iota = lax.broadcasted_iota(jnp.int32, s.shape, 0)
    c_iota = lax.broadcasted_iota(jnp.int32, s.shape, 1)
    s = jnp.where(c_iota <= (r_iota & (tq - 1)), s, NEG_INF)
    m = jnp.max(s, axis=1, keepdims=True)
    p = jnp.exp(s - m)
    l = jnp.sum(p, axis=1, keepdims=True)
    acc = _dot(p.astype(BF16), v_ref[pl.ds(start, tq), :])

    e_lane = lax.broadcasted_iota(jnp.int32, (MOBA_BLOCK, LANES), 1)

    def body(n, carry):
        m, l, acc = carry
        off = pl.multiple_of(n * MOBA_BLOCK, MOBA_BLOCK)
        kb = k_ref[pl.ds(off, MOBA_BLOCK), :]
        vb = v_ref[pl.ds(off, MOBA_BLOCK), :]
        onehot = jnp.where((e_lane == n) | (e_lane == n + nb), 1.0, 0.0).astype(BF16)
        s = _dot_nt(qa, jnp.concatenate([kb, onehot], axis=1))
        m_new = jnp.maximum(m, jnp.max(s, axis=1, keepdims=True))
        alpha = jnp.exp(m - m_new)
        p = jnp.exp(s - m_new)
        l = alpha * l + jnp.sum(p, axis=1, keepdims=True)
        acc = alpha * acc + _dot(p.astype(BF16), vb)
        return m_new, l, acc

    m, l, acc = lax.fori_loop(0, qi, body, (m, l, acc))
    out = acc / l
    o_ref[...] = jnp.where(lane < HEAD_DIM, out[:tq], out[tq:]).astype(o_ref.dtype)


def _attn_prompt(qb, bias, kb, vb, *, batch):
    t, d = qb.shape
    s = t // batch
    tq = MOBA_BLOCK
    tiles = s // tq
    n_pairs = N_HEADS // HEADS_PER_TILE
    tile_spec = pl.BlockSpec((tq, LANES), lambda b, hp, i: (b * tiles + i, hp))
    seq_spec = pl.BlockSpec((s, LANES), lambda b, hp, i: (b, hp))
    return pl.pallas_call(
        _attn_prompt_kernel,
        grid=(batch, n_pairs, tiles),
        in_specs=[tile_spec, tile_spec, seq_spec, seq_spec],
        out_specs=tile_spec,
        out_shape=jax.ShapeDtypeStruct((t, d), BF16),
        compiler_params=_params(3),
        name="attn_prompt",
    )(qb, bias, kb, vb)


PAGES_PER_STEP = 16


def _page_sums_kernel(pt_ref, *refs):
    del pt_ref
    page_refs, o_ref = refs[:PAGES_PER_STEP], refs[PAGES_PER_STEP]
    step = pl.program_id(1)
    blocks_per_step = PAGES_PER_STEP // PAGES_PER_BLOCK

    @pl.when(step == 0)
    def _():
        o_ref[...] = jnp.zeros_like(o_ref)

    lane = lax.broadcasted_iota(jnp.int32, o_ref.shape[1:], 2)
    acc = o_ref[0]
    for blk in range(blocks_per_step):
        tot = page_refs[blk * PAGES_PER_BLOCK][0]
        for j in range(1, PAGES_PER_BLOCK):
            tot = tot + page_refs[blk * PAGES_PER_BLOCK + j][0]
        col = jnp.sum(tot, axis=-1, keepdims=True)
        acc = jnp.where(lane == step * blocks_per_step + blk, col, acc)
    o_ref[0] = acc


def _block_key_sums(cache_kt, page_table, layer_page0):
    b, n_pages = page_table.shape
    _, h, dh, page = cache_kt.shape
    assert n_pages % PAGES_PER_STEP == 0 and n_pages // PAGES_PER_BLOCK < LANES

    def page_spec(j):
        return pl.BlockSpec(
            (1, h, dh, page),
            lambda bi, i, pt: (layer_page0 + pt[bi, i * PAGES_PER_STEP + j], 0, 0, 0))

    return pl.pallas_call(
        _page_sums_kernel,
        grid_spec=pltpu.PrefetchScalarGridSpec(
            num_scalar_prefetch=1,
            grid=(b, n_pages // PAGES_PER_STEP),
            in_specs=[page_spec(j) for j in range(PAGES_PER_STEP)],
            out_specs=pl.BlockSpec((1, h, dh, LANES), lambda bi, i, pt: (bi, 0, 0, 0)),
        ),
        out_shape=jax.ShapeDtypeStruct((b, h, dh, LANES), F32),
        compiler_params=_params(2),
        name="block_key_sums",
    )(page_table, *([cache_kt] * PAGES_PER_STEP))


def _as_column(row):
    d = row.shape[1]
    return jnp.concatenate(
        [jnp.broadcast_to(row[:, c * LANES:(c + 1) * LANES], (LANES, LANES)).T
         for c in range(d // LANES)], axis=0)


def _select_sample_kernel(q_ref, knew_ref, sums_ref, o_ref, *, nb):
    _, h, dh, _ = sums_ref.shape
    lane3 = lax.broadcasted_iota(jnp.int32, (h, dh, LANES), 2)
    q_col = _as_column(q_ref[0]).reshape(h, dh, LANES)
    sums = jnp.where(lane3 == nb, _as_column(knew_ref[0]).reshape(h, dh, LANES), sums_ref[0])
    scores = jnp.sum(sums * (1.0 / MOBA_BLOCK) * q_col, axis=1)
    n_iota = lax.broadcasted_iota(jnp.int32, scores.shape, 1)
    s = jnp.where(n_iota < nb, scores, NEG_INF)
    picks = jnp.zeros(scores.shape, jnp.int32)
    for r in range(MOBA_TOPK):
        best = jnp.max(s, axis=1, keepdims=True)
        idx = jnp.min(jnp.where(s == best, n_iota, LANES), axis=1, keepdims=True)
        s = jnp.where(n_iota == idx, BELOW_NEG_INF, s)
        picks = jnp.where(n_iota == r, idx, picks)
    o_ref[0] = picks


def _select_sample(q, k_new, sums, nb):
    b, h, dh, _ = sums.shape
    d = h * dh
    return pl.pallas_call(
        functools.partial(_select_sample_kernel, nb=nb),
        grid=(b,),
        in_specs=[pl.BlockSpec((1, 1, d), lambda i: (i, 0, 0)),
                  pl.BlockSpec((1, 1, d), lambda i: (i, 0, 0)),
                  pl.BlockSpec((1, h, dh, LANES), lambda i: (i, 0, 0, 0))],
        out_specs=pl.BlockSpec((1, h, LANES), lambda i: (i, 0, 0)),
        out_shape=jax.ShapeDtypeStruct((b, h, LANES), jnp.int32),
        compiler_params=_params(1),
        name="select_sample",
    )(q, k_new, sums)


N_SEL_PAGES = HEADS_PER_TILE * MOBA_TOPK * PAGES_PER_BLOCK


def _attn_sample_kernel(idx_ref, pt_ref, q_ref, knew_ref, vnew_ref, *refs):
    del idx_ref, pt_ref
    k_refs, v_refs, o_ref = refs[:N_SEL_PAGES], refs[N_SEL_PAGES:2 * N_SEL_PAGES], refs[-1]
    rows = 8
    q = jnp.broadcast_to(q_ref[0] * (HEAD_DIM ** -0.5), (rows, LANES))
    k_new = jnp.broadcast_to(knew_ref[0], (rows, LANES))
    v_new = jnp.broadcast_to(vnew_ref[0], (rows, LANES))
    per_head = MOBA_TOPK * PAGES_PER_BLOCK
    outs = []
    for hh in range(HEADS_PER_TILE):
        cols = slice(hh * HEAD_DIM, (hh + 1) * HEAD_DIM)
        qh = q[:, cols]
        qh_b = qh.astype(BF16)
        logits = [_dot(qh_b, k_refs[hh * per_head + c][0, 0].astype(BF16)) for c in range(per_head)]
        own = jnp.sum(qh * k_new[:, cols], axis=1, keepdims=True)
        m = own
        for lg in logits:
            m = jnp.maximum(m, jnp.max(lg, axis=1, keepdims=True))
        p_own = jnp.exp(own - m)
        denom = p_own
        acc = p_own * v_new[:, cols]
        for c, lg in enumerate(logits):
            p = jnp.exp(lg - m)
            denom = denom + jnp.sum(p, axis=1, keepdims=True)
            acc = acc + _dot_nt(p.astype(BF16), v_refs[hh * per_head + c][0, 0].astype(BF16))
        outs.append(acc / denom)
    o_ref[0] = jnp.concatenate(outs, axis=1)[:1]


def _attn_sample(q, k_new, v_new, cache_kt, cache_vt, sel_idx, page_table, layer_page0):
    b, _, d = q.shape
    _, _, dh, page = cache_kt.shape
    n_pages = page_table.shape[0] // b
    n_pairs = N_HEADS // HEADS_PER_TILE
    vec_spec = pl.BlockSpec((1, 1, LANES), lambda bi, hp, idx, pt: (bi, 0, hp))

    def page_spec(hh, r, j):
        def index_map(bi, hp, idx, pt):
            head = hp * HEADS_PER_TILE + hh
            blk = idx[(bi * N_HEADS + head) * MOBA_TOPK + r]
            return (layer_page0 + pt[bi * n_pages + blk * PAGES_PER_BLOCK + j], head, 0, 0)
        return pl.BlockSpec((1, 1, dh, page), index_map)

    page_specs = [page_spec(hh, r, j) for hh in range(HEADS_PER_TILE) for r in range(MOBA_TOPK)
                  for j in range(PAGES_PER_BLOCK)]
    return pl.pallas_call(
        _attn_sample_kernel,
        grid_spec=pltpu.PrefetchScalarGridSpec(
            num_scalar_prefetch=2,
            grid=(b, n_pairs),
            in_specs=[vec_spec] * 3 + page_specs * 2,
            out_specs=vec_spec,
        ),
        out_shape=jax.ShapeDtypeStruct((b, 1, d), F32),
        compiler_params=_params(2),
        name="attn_sample",
    )(sel_idx, page_table, q, k_new, v_new,
      *([cache_kt] * N_SEL_PAGES), *([cache_vt] * N_SEL_PAGES))


FFN_TILE = 1024
GMLP_TILE = 512
PROJ_TILE = 512


def kernel(x_prompt, x_sample, cache_k, cache_v, page_table, norm_ffn1, ffn1_gate, ffn1_up, ffn1_down, norm_mix, norm_ffn2, ffn2_gate, ffn2_up, ffn2_down, gmlp_in, gmlp_vnorm_g, gmlp_vnorm_b, gmlp_ws, gmlp_bs, gmlp_out, moba_qkv, moba_out, final_norm):
    bp, s, d = x_prompt.shape
    bs, t_new, _ = x_sample.shape
    assert t_new == 1 and s % MOBA_BLOCK == 0 and d == N_HEADS * HEAD_DIM
    depth = norm_ffn1.shape[0]
    n_pool, page = cache_k.shape[1], cache_k.shape[2]
    assert page == PAGE_SIZE
    nb_prompt = s // MOBA_BLOCK

    yp = x_prompt.reshape(bp * s, d)
    ys = x_sample.reshape(bs * t_new, d)
    cache_kt = jnp.transpose(cache_k, (0, 1, 3, 4, 2)).reshape(-1, N_HEADS, HEAD_DIM, PAGE_SIZE)
    cache_vt = jnp.transpose(cache_v, (0, 1, 3, 4, 2)).reshape(-1, N_HEADS, HEAD_DIM, PAGE_SIZE)
    nb_past = page_table.shape[1] // PAGES_PER_BLOCK
    head_of_col = jnp.arange(d) // HEAD_DIM

    sv_p, sv_s, kp_l, vp_l, ks_l, vs_l = [], [], [], [], [], []
    for i in range(depth):
        last = i == depth - 1
        wg, wu, wd = ffn1_gate[i].astype(BF16), ffn1_up[i].astype(BF16), ffn1_down[i].astype(BF16)
        yp = _ffn(yp, norm_ffn1[i], wg, wu, wd, tm=FFN_TILE)
        ys = _ffn(ys, norm_ffn1[i], wg, wu, wd, tm=bs)
        j = i // 2
        if i % 2 == 0:
            w_in, w_out = gmlp_in[j].astype(BF16), gmlp_out[j].astype(BF16)
            yp, v_p = _gmlp_prompt(yp, norm_mix[i], w_in, gmlp_vnorm_g[j], gmlp_vnorm_b[j],
                                   gmlp_ws[j], gmlp_bs[j], w_out, batch=bp, tm=GMLP_TILE)
            ys, v_s = _gmlp_sample(ys, norm_mix[i], w_in, gmlp_vnorm_g[j], gmlp_vnorm_b[j],
                                   gmlp_ws[j], gmlp_bs[j], w_out)
            sv_p.append(v_p.reshape(bp, CHUNK, GMLP_GROUPS, -1))
            sv_s.append(v_s.reshape(bs, t_new, GMLP_GROUPS, -1))
        else:
            w_qkv, w_o = moba_qkv[j].astype(BF16), moba_out[j].astype(BF16)
            kf, vf, qb, kb, vb, means = _qkv_prompt(yp, norm_mix[i], w_qkv)
            means = means.reshape(bp, nb_prompt, d)
            means_t = jnp.where(jnp.arange(N_HEADS)[None, :, None, None] == head_of_col[None, None, None, :],
                                means[:, None, :, :], 0.0).reshape(bp, N_HEADS * nb_prompt, d).astype(BF16)
            bias = _select_prompt(qb, means_t, batch=bp)
            attn_p = _attn_prompt(qb, bias, kb, vb, batch=bp)
            yp = _proj_residual(yp, attn_p, w_o, tm=PROJ_TILE)
            kp_l.append(kf.reshape(bp, s, N_HEADS, HEAD_DIM))
            vp_l.append(vf.reshape(bp, s, N_HEADS, HEAD_DIM))
            qs, kn, vn = _qkv_sample(ys, norm_mix[i], w_qkv)
            qs3, kn3, vn3 = qs.reshape(bs, 1, d), kn.reshape(bs, 1, d), vn.reshape(bs, 1, d)
            sums = _block_key_sums(cache_kt, page_table, j * n_pool)
            picks = _select_sample(qs3, kn3, sums, nb_past)
            sel_idx = picks[:, :, :MOBA_TOPK].reshape(-1)
            attn_s = _attn_sample(qs3, kn3, vn3, cache_kt, cache_vt, sel_idx,
                                  page_table.reshape(-1), j * n_pool)
            ys = _proj_residual(ys, attn_s.reshape(bs, d), w_o, tm=bs)
            ks_l.append(kn.reshape(bs, t_new, N_HEADS, HEAD_DIM))
            vs_l.append(vn.reshape(bs, t_new, N_HEADS, HEAD_DIM))
        wg, wu, wd = ffn2_gate[i].astype(BF16), ffn2_up[i].astype(BF16), ffn2_down[i].astype(BF16)
        fg = final_norm if last else None
        yp = _ffn(yp, norm_ffn2[i], wg, wu, wd, fg, tm=FFN_TILE)
        ys = _ffn(ys, norm_ffn2[i], wg, wu, wd, fg, tm=bs)
    return (yp.reshape(bp, s, d), ys.reshape(bs, t_new, d), jnp.stack(sv_p), jnp.stack(sv_s),
            jnp.stack(kp_l), jnp.stack(vp_l), jnp.stack(ks_l), jnp.stack(vs_l))
```

```python
import functools

import jax
import jax.numpy as jnp
from jax import lax
from jax.experimental import pallas as pl
from jax.experimental.pallas import tpu as pltpu

F32 = jnp.float32
BF16 = jnp.bfloat16

EPS = 1e-6
NEG_INF = -1e30
BELOW_NEG_INF = -3e38

CHUNK = 128
GMLP_GROUPS = 8
N_HEADS = 16
HEAD_DIM = 64
MOBA_BLOCK = 256
MOBA_TOPK = 3
PAGE_SIZE = 128
PAGES_PER_BLOCK = MOBA_BLOCK // PAGE_SIZE

LANES = 128
HEADS_PER_TILE = LANES // HEAD_DIM
VMEM_LIMIT_BYTES = 56 * 1024 * 1024


def _params(n_axes):
    return pltpu.CompilerParams(dimension_semantics=("arbitrary",) * n_axes,
                                vmem_limit_bytes=VMEM_LIMIT_BYTES)


def _rms(x, g):
    return x * lax.rsqrt(jnp.mean(x * x, axis=-1, keepdims=True) + EPS) * g


def _gelu_tanh(x):
    return 0.5 * x * (1.0 + jnp.tanh(0.7978845608028654 * (x + 0.044715 * (x * x * x))))


def _dot(a, b):
    return jnp.dot(a, b, preferred_element_type=F32)


def _dot_nt(a, b):
    return lax.dot_general(a, b, (((1,), (1,)), ((), ())), preferred_element_type=F32)


def _ffn_kernel(x_ref, g_ref, wg_ref, wu_ref, wd_ref, *rest, final_norm):
    if final_norm:
        fg_ref, o_ref, h_ref, acc_ref = rest
    else:
        o_ref, h_ref, acc_ref = rest
    j = pl.program_id(1)

    @pl.when(j == 0)
    def _():
        h_ref[...] = _rms(x_ref[...], g_ref[...]).astype(BF16)
        acc_ref[...] = jnp.zeros_like(acc_ref)

    h = h_ref[...]
    gate = _dot(h, wg_ref[...])
    up = _dot(h, wu_ref[...])
    act = (gate * jax.nn.sigmoid(gate) * up).astype(BF16)
    acc_ref[...] += _dot(act, wd_ref[...])

    @pl.when(j == pl.num_programs(1) - 1)
    def _():
        y = x_ref[...] + 0.5 * acc_ref[...]
        if final_norm:
            y = _rms(y, fg_ref[...])
        o_ref[...] = y


def _ffn(x, g, wg, wu, wd, final_g=None, *, tm, tf=256):
    t, d = x.shape
    ff = wg.shape[1]
    in_specs = [
        pl.BlockSpec((tm, d), lambda i, j: (i, 0)),
        pl.BlockSpec((1, d), lambda i, j: (0, 0)),
        pl.BlockSpec((d, tf), lambda i, j: (0, j)),
        pl.BlockSpec((d, tf), lambda i, j: (0, j)),
        pl.BlockSpec((tf, d), lambda i, j: (j, 0)),
    ]
    args = [x, g.reshape(1, d), wg, wu, wd]
    if final_g is not None:
        in_specs.append(pl.BlockSpec((1, d), lambda i, j: (0, 0)))
        args.append(final_g.reshape(1, d))
    return pl.pallas_call(
        functools.partial(_ffn_kernel, final_norm=final_g is not None),
        grid=(t // tm, ff // tf),
        in_specs=in_specs,
        out_specs=pl.BlockSpec((tm, d), lambda i, j: (i, 0)),
        out_shape=jax.ShapeDtypeStruct((t, d), F32),
        scratch_shapes=[pltpu.VMEM((tm, d), BF16), pltpu.VMEM((tm, d), F32)],
        compiler_params=_params(2),
        name="ffn",
    )(*args)


def _group_layernorm(v, gain, bias):
    mu = jnp.mean(v, axis=-1, keepdims=True)
    d = v - mu
    var = jnp.mean(d * d, axis=-1, keepdims=True)
    return d * lax.rsqrt(var + EPS) * gain + bias


def _gmlp_prompt_kernel(x_ref, g_ref, win_ref, vg_ref, vb_ref, ws_ref, bs_ref, wout_ref,
                        o_ref, sv_ref):
    tm, d = x_ref.shape
    width = win_ref.shape[1] // 2
    gd = width // GMLP_GROUPS
    n_chunks = tm // CHUNK
    x = x_ref[...]
    h = _rms(x, g_ref[...]).astype(BF16)
    z = _gelu_tanh(_dot(h, win_ref[...]))
    row = lax.broadcasted_iota(jnp.int32, (CHUNK, CHUNK), 0)
    col = lax.broadcasted_iota(jnp.int32, (CHUNK, CHUNK), 1)
    is_last = pl.program_id(1) == pl.num_programs(1) - 1
    gated = []
    for grp in range(GMLP_GROUPS):
        lanes = slice(grp * gd, (grp + 1) * gd)
        vn = _group_layernorm(z[:, width + grp * gd: width + (grp + 1) * gd],
                              vg_ref[:, lanes], vb_ref[:, lanes])

        @pl.when(is_last)
        def _(vn=vn, lanes=lanes):
            sv_ref[0, :, lanes] = vn[tm - CHUNK:, :]

        ws = jnp.where(row >= col, ws_ref[grp], 0.0).astype(BF16)
        rhs = jnp.concatenate([vn[c * CHUNK:(c + 1) * CHUNK] for c in range(n_chunks)],
                              axis=1).astype(BF16)
        mixed_cat = _dot(ws, rhs)
        mixed = jnp.concatenate(
            [mixed_cat[:, c * gd:(c + 1) * gd] + bs_ref[grp] for c in range(n_chunks)], axis=0)
        gated.append((z[:, lanes] * mixed).astype(BF16))
    a = jnp.concatenate(gated, axis=1)
    o_ref[...] = x + _dot(a, wout_ref[...])


def _gmlp_prompt(x, g, w_in, vn_g, vn_b, w_s, b_s, w_out, *, batch, tm):
    t, d = x.shape
    width = w_out.shape[0]
    gd = width // GMLP_GROUPS
    tiles = t // batch // tm
    bs_b = jnp.broadcast_to(b_s[:, :, None], (GMLP_GROUPS, CHUNK, gd))
    const2 = lambda b, i: (0, 0)
    const3 = lambda b, i: (0, 0, 0)
    return pl.pallas_call(
        _gmlp_prompt_kernel,
        grid=(batch, tiles),
        in_specs=[
            pl.BlockSpec((tm, d), lambda b, i: (b * tiles + i, 0)),
            pl.BlockSpec((1, d), const2),
            pl.BlockSpec((d, 2 * width), const2),
            pl.BlockSpec((1, width), const2),
            pl.BlockSpec((1, width), const2),
            pl.BlockSpec((GMLP_GROUPS, CHUNK, CHUNK), const3),
            pl.BlockSpec((GMLP_GROUPS, CHUNK, gd), const3),
            pl.BlockSpec((width, d), const2),
        ],
        out_specs=[
            pl.BlockSpec((tm, d), lambda b, i: (b * tiles + i, 0)),
            pl.BlockSpec((1, CHUNK, width), lambda b, i: (b, 0, 0)),
        ],
        out_shape=[jax.ShapeDtypeStruct((t, d), F32),
                   jax.ShapeDtypeStruct((batch, CHUNK, width), F32)],
        compiler_params=_params(2),
        name="gmlp_prompt",
    )(x, g.reshape(1, d), w_in, vn_g.reshape(1, width), vn_b.reshape(1, width), w_s, bs_b, w_out)


def _gmlp_sample_kernel(x_ref, g_ref, win_ref, vg_ref, vb_ref, coef_ref, bias_ref, wout_ref,
                        o_ref, sv_ref):
    width = win_ref.shape[1] // 2
    gd = width // GMLP_GROUPS
    x = x_ref[...]
    h = _rms(x, g_ref[...]).astype(BF16)
    z = _gelu_tanh(_dot(h, win_ref[...]))
    vn = jnp.concatenate(
        [_group_layernorm(z[:, width + grp * gd: width + (grp + 1) * gd],
                          vg_ref[:, grp * gd:(grp + 1) * gd], vb_ref[:, grp * gd:(grp + 1) * gd])
         for grp in range(GMLP_GROUPS)], axis=1)
    sv_ref[...] = vn
    mixed = vn * coef_ref[...] + bias_ref[...]
    o_ref[...] = x + _dot((z[:, :width] * mixed).astype(BF16), wout_ref[...])


def _gmlp_sample(x, g, w_in, vn_g, vn_b, w_s, b_s, w_out):
    t, d = x.shape
    width = w_out.shape[0]
    gd = width // GMLP_GROUPS
    coef = jnp.repeat(w_s[:, 0, 0], gd).reshape(1, width)
    bias = jnp.repeat(b_s[:, 0], gd).reshape(1, width)
    return pl.pallas_call(
        _gmlp_sample_kernel,
        out_shape=[jax.ShapeDtypeStruct((t, d), F32), jax.ShapeDtypeStruct((t, width), F32)],
        compiler_params=pltpu.CompilerParams(vmem_limit_bytes=VMEM_LIMIT_BYTES),
        name="gmlp_sample",
    )(x, g.reshape(1, d), w_in, vn_g.reshape(1, width), vn_b.reshape(1, width), coef, bias, w_out)


LOG2_E = 1.4426950408889634
Q_SCALE_LOG2 = HEAD_DIM ** -0.5 * LOG2_E


def _qkv_prompt_kernel(x_ref, g_ref, wt_ref, wk_ref, ktf_ref, vtf_ref, qtb_ref, kab_ref, vtb_ref,
                       mean_ref):
    tm, d = x_ref.shape
    n_pairs = d // LANES
    h = _rms(x_ref[...], g_ref[...]).astype(BF16)
    qkv_t = _dot_nt(wt_ref[...], h)
    ktf_ref[0] = qkv_t[d:2 * d]
    vtf_ref[0] = qkv_t[2 * d:]
    qtb_ref[0, 0] = (qkv_t[:d] * Q_SCALE_LOG2).astype(BF16)
    vtb_ref[0, 0] = qkv_t[2 * d:].astype(BF16)
    k = _dot(h, wk_ref[...])
    mean_ref[0] = jnp.sum(k, axis=0, keepdims=True) * (1.0 / MOBA_BLOCK)
    kb = k.astype(BF16)
    lane = lax.broadcasted_iota(jnp.int32, (tm, LANES), 1)
    onehot = jnp.where(lane == pl.program_id(1), 1.0, 0.0).astype(BF16)
    pieces = []
    for hp in range(n_pairs):
        pieces += [kb[:, hp * LANES:(hp + 1) * LANES], onehot]
    kab_ref[...] = jnp.concatenate(pieces, axis=1)


def _qkv_prompt(x, g, w_t, w_k, *, batch):
    t, d = x.shape
    tm = MOBA_BLOCK
    nb = t // batch // tm
    assert nb <= LANES
    const = lambda b, i: (0, 0)
    feat_spec = pl.BlockSpec((1, d, tm), lambda b, i: (b, 0, i))
    blk_spec = pl.BlockSpec((1, 1, d, tm), lambda b, i: (b, i, 0, 0))
    return pl.pallas_call(
        _qkv_prompt_kernel,
        grid=(batch, nb),
        in_specs=[pl.BlockSpec((tm, d), lambda b, i: (b * nb + i, 0)), pl.BlockSpec((1, d), const),
                  pl.BlockSpec((3 * d, d), const), pl.BlockSpec((d, d), const)],
        out_specs=[feat_spec, feat_spec, blk_spec,
                   pl.BlockSpec((tm, 2 * d), lambda b, i: (b * nb + i, 0)), blk_spec,
                   pl.BlockSpec((1, 1, d), lambda b, i: (b * nb + i, 0, 0))],
        out_shape=[jax.ShapeDtypeStruct((batch, d, nb * tm), F32)] * 2
        + [jax.ShapeDtypeStruct((batch, nb, d, tm), BF16), jax.ShapeDtypeStruct((t, 2 * d), BF16),
           jax.ShapeDtypeStruct((batch, nb, d, tm), BF16), jax.ShapeDtypeStruct((batch * nb, 1, d), F32)],
        compiler_params=_params(2),
        name="qkv_prompt",
    )(x, g.reshape(1, d), w_t, w_k)


def _qkv_sample_kernel(x_ref, g_ref, w_ref, q_ref, k_ref, v_ref):
    d = x_ref.shape[1]
    h = _rms(x_ref[...], g_ref[...]).astype(BF16)
    qkv = _dot(h, w_ref[...])
    q_ref[...] = qkv[:, :d]
    k_ref[...] = qkv[:, d:2 * d]
    v_ref[...] = qkv[:, 2 * d:]


def _qkv_sample(x, g, w):
    t, d = x.shape
    return pl.pallas_call(
        _qkv_sample_kernel,
        out_shape=[jax.ShapeDtypeStruct((t, d), F32)] * 3,
        compiler_params=pltpu.CompilerParams(vmem_limit_bytes=VMEM_LIMIT_BYTES),
        name="qkv_sample",
    )(x, g.reshape(1, d), w)


def _proj_kernel(x_ref, a_ref, w_ref, o_ref):
    o_ref[...] = x_ref[...] + _dot(a_ref[...].astype(BF16), w_ref[...])


def _proj_residual(x, a, w, *, tm):
    t, d = x.shape
    row_spec = pl.BlockSpec((tm, d), lambda i: (i, 0))
    return pl.pallas_call(
        _proj_kernel,
        grid=(t // tm,),
        in_specs=[row_spec, row_spec, pl.BlockSpec((d, d), lambda i: (0, 0))],
        out_specs=row_spec,
        out_shape=jax.ShapeDtypeStruct((t, d), F32),
        compiler_params=_params(1),
        name="proj_residual",
    )(x, a, w)


def _top_k_mask(scores, n_iota, n_valid):
    axis = scores.ndim - 2
    n_total = scores.shape[axis]
    s = jnp.where(n_iota < n_valid, scores, NEG_INF)
    picked = jnp.zeros(scores.shape, F32)
    idxs = []
    for _ in range(MOBA_TOPK):
        best = jnp.max(s, axis=axis, keepdims=True)
        idx = jnp.min(jnp.where(s == best, n_iota, n_total), axis=axis, keepdims=True)
        hit = n_iota == idx
        picked = jnp.where(hit, 1.0, picked)
        s = jnp.where(hit, BELOW_NEG_INF, s)
        idxs.append(idx)
    return picked, idxs


def _select_prompt_kernel(qt_ref, mt_ref, o_ref):
    tq = qt_ref.shape[3]
    nb = mt_ref.shape[1] // N_HEADS
    own = pl.program_id(1)
    st = _dot(mt_ref[0], qt_ref[0, 0])
    st = st.reshape(N_HEADS, nb, tq)
    n_iota = lax.broadcasted_iota(jnp.int32, st.shape, 1)
    picked, _ = _top_k_mask(st, n_iota, own)
    bias = jnp.where(n_iota < own, jnp.where(picked > 0.0, 0.0, NEG_INF), NEG_INF)
    o_ref[0] = bias.astype(BF16)


def _select_prompt(qt, means_t):
    batch, nb, d, tq = qt.shape
    return pl.pallas_call(
        _select_prompt_kernel,
        grid=(batch, nb),
        in_specs=[pl.BlockSpec((1, 1, d, tq), lambda b, i: (b, i, 0, 0)),
                  pl.BlockSpec((1, N_HEADS * nb, d), lambda b, i: (b, 0, 0))],
        out_specs=pl.BlockSpec((1, N_HEADS, nb, tq), lambda b, i: (b, 0, 0, i)),
        out_shape=jax.ShapeDtypeStruct((batch, N_HEADS, nb, nb * tq), BF16),
        compiler_params=_params(2),
        name="select_prompt",
    )(qt, means_t)


PAIRS_PER_STEP = 2
BLOCKS_PER_TRIP = 4


def _attn_prompt_kernel(qt_ref, bias_ref, ka_ref, vt_ref, o_ref, qa_ref):
    tq = qt_ref.shape[3]
    nb = bias_ref.shape[2]
    qi = pl.program_id(2)
    n_chains = PAIRS_PER_STEP * HEADS_PER_TILE
    feat = lax.broadcasted_iota(jnp.int32, (LANES, tq), 0)
    zeros_bias_pad = jnp.zeros((LANES - nb, tq), BF16)
    for c in range(n_chains):
        p, hh = divmod(c, HEADS_PER_TILE)
        qt = qt_ref[0, 0, p * LANES:(p + 1) * LANES, :]
        in_head = (feat >= hh * HEAD_DIM) & (feat < (hh + 1) * HEAD_DIM)
        qa_ref[c, :LANES, :] = jnp.where(in_head, qt, jnp.zeros_like(qt))
        qa_ref[c, LANES:, :] = jnp.concatenate([bias_ref[0, c], zeros_bias_pad], axis=0)

    def v_rows(c, blk):
        p, hh = divmod(c, HEADS_PER_TILE)
        r0 = p * LANES + hh * HEAD_DIM
        return vt_ref[0, blk, r0:r0 + HEAD_DIM, :]


    start = pl.multiple_of(qi * tq, tq)
    key_pos = lax.broadcasted_iota(jnp.int32, (tq, tq), 0)
    qry_pos = lax.broadcasted_iota(jnp.int32, (tq, tq), 1)
    scores = []
    for c in range(n_chains):
        p = c // HEADS_PER_TILE
        kd = ka_ref[pl.ds(start, tq), p * 2 * LANES:p * 2 * LANES + LANES]
        scores.append(_dot(kd, qa_ref[c, :LANES, :]))
    state, probs = [], []
    for c in range(n_chains):
        s = jnp.where(key_pos <= qry_pos, scores[c], NEG_INF)
        m = jnp.max(s, axis=0, keepdims=True)
        e = jnp.exp2(s - m)
        state += [m, jnp.sum(e, axis=0, keepdims=True)]
        probs.append(e.astype(BF16))
    state = [x for c in range(n_chains)
             for x in (state[2 * c], state[2 * c + 1], _dot(v_rows(c, qi), probs[c]))]

    def body(t, state):
        blocks = [t * BLOCKS_PER_TRIP + j for j in range(BLOCKS_PER_TRIP)]
        scores = []
        for c in range(n_chains):
            p = c // HEADS_PER_TILE
            for n in blocks:
                off = pl.multiple_of(n * MOBA_BLOCK, MOBA_BLOCK)
                ka = ka_ref[pl.ds(off, MOBA_BLOCK), p * 2 * LANES:(p + 1) * 2 * LANES]
                scores.append(_dot(ka, qa_ref[c]))
        probs, new_state = [], []
        for c in range(n_chains):
            m, l, acc = state[3 * c:3 * c + 3]
            ss = scores[c * BLOCKS_PER_TRIP:(c + 1) * BLOCKS_PER_TRIP]
            m_new = m
            for s in ss:
                m_new = jnp.maximum(m_new, jnp.max(s, axis=0, keepdims=True))
            alpha = jnp.exp2(m - m_new)
            l = alpha * l
            for s in ss:
                e = jnp.exp2(s - m_new)
                l = l + jnp.sum(e, axis=0, keepdims=True)
                probs.append(e.astype(BF16))
            new_state += [m_new, l, alpha * acc]
        for c in range(n_chains):
            for j, n in enumerate(blocks):
                new_state[3 * c + 2] = new_state[3 * c + 2] + _dot(
                    v_rows(c, n), probs[c * BLOCKS_PER_TRIP + j])
        return tuple(new_state)

    n_trips = (qi + BLOCKS_PER_TRIP - 1) // BLOCKS_PER_TRIP
    state = lax.fori_loop(0, n_trips, body, tuple(state))
    outs = []
    for c in range(n_chains):
        _, l, acc = state[3 * c:3 * c + 3]
        outs.append((acc / l).T)
    o_ref[...] = jnp.concatenate(outs, axis=1).astype(o_ref.dtype)


def _attn_prompt(qt, bias, ka, vt):
    batch, nb, d, tq = qt.shape
    s = nb * tq
    n_groups = d // (PAIRS_PER_STEP * LANES)
    gw = PAIRS_PER_STEP * LANES
    n_chains = PAIRS_PER_STEP * HEADS_PER_TILE
    return pl.pallas_call(
        _attn_prompt_kernel,
        grid=(batch, n_groups, nb),
        in_specs=[pl.BlockSpec((1, 1, gw, tq), lambda b, g, i: (b, i, g, 0)),
                  pl.BlockSpec((1, n_chains, nb, tq), lambda b, g, i: (b, g, 0, i)),
                  pl.BlockSpec((s, 2 * gw), lambda b, g, i: (b, g)),
                  pl.BlockSpec((1, nb, gw, tq), lambda b, g, i: (b, 0, g, 0))],
        out_specs=pl.BlockSpec((tq, gw), lambda b, g, i: (b * nb + i, g)),
        out_shape=jax.ShapeDtypeStruct((batch * s, d), BF16),
        scratch_shapes=[pltpu.VMEM((n_chains, 2 * LANES, tq), BF16)],
        compiler_params=_params(3),
        name="attn_prompt",
    )(qt, bias, ka, vt)


PAGES_PER_STEP = 16


def _page_sums_kernel(pt_ref, *refs):
    del pt_ref
    page_refs, o_ref = refs[:PAGES_PER_STEP], refs[PAGES_PER_STEP]
    step = pl.program_id(1)
    blocks_per_step = PAGES_PER_STEP // PAGES_PER_BLOCK

    @pl.when(step == 0)
    def _():
        o_ref[...] = jnp.zeros_like(o_ref)

    lane = lax.broadcasted_iota(jnp.int32, o_ref.shape[1:], 2)
    acc = o_ref[0]
    for blk in range(blocks_per_step):
        tot = page_refs[blk * PAGES_PER_BLOCK][0]
        for j in range(1, PAGES_PER_BLOCK):
            tot = tot + page_refs[blk * PAGES_PER_BLOCK + j][0]
        col = jnp.sum(tot, axis=-1, keepdims=True)
        acc = jnp.where(lane == step * blocks_per_step + blk, col, acc)
    o_ref[0] = acc


def _block_key_sums(cache_kt, page_table, layer_page0):
    b, n_pages = page_table.shape
    _, h, dh, page = cache_kt.shape
    assert n_pages % PAGES_PER_STEP == 0 and n_pages // PAGES_PER_BLOCK < LANES

    def page_spec(j):
        return pl.BlockSpec(
            (1, h, dh, page),
            lambda bi, i, pt: (layer_page0 + pt[bi, i * PAGES_PER_STEP + j], 0, 0, 0))

    return pl.pallas_call(
        _page_sums_kernel,
        grid_spec=pltpu.PrefetchScalarGridSpec(
            num_scalar_prefetch=1,
            grid=(b, n_pages // PAGES_PER_STEP),
            in_specs=[page_spec(j) for j in range(PAGES_PER_STEP)],
            out_specs=pl.BlockSpec((1, h, dh, LANES), lambda bi, i, pt: (bi, 0, 0, 0)),
        ),
        out_shape=jax.ShapeDtypeStruct((b, h, dh, LANES), F32),
        compiler_params=_params(2),
        name="block_key_sums",
    )(page_table, *([cache_kt] * PAGES_PER_STEP))


def _as_column(row):
    d = row.shape[1]
    return jnp.concatenate(
        [jnp.broadcast_to(row[:, c * LANES:(c + 1) * LANES], (LANES, LANES)).T
         for c in range(d // LANES)], axis=0)


def _select_sample_kernel(q_ref, knew_ref, sums_ref, o_ref, *, nb):
    _, h, dh, _ = sums_ref.shape
    lane3 = lax.broadcasted_iota(jnp.int32, (h, dh, LANES), 2)
    q_col = _as_column(q_ref[0]).reshape(h, dh, LANES)
    sums = jnp.where(lane3 == nb, _as_column(knew_ref[0]).reshape(h, dh, LANES), sums_ref[0])
    scores = jnp.sum(sums * (1.0 / MOBA_BLOCK) * q_col, axis=1)
    n_iota = lax.broadcasted_iota(jnp.int32, scores.shape, 1)
    s = jnp.where(n_iota < nb, scores, NEG_INF)
    picks = jnp.zeros(scores.shape, jnp.int32)
    for r in range(MOBA_TOPK):
        best = jnp.max(s, axis=1, keepdims=True)
        idx = jnp.min(jnp.where(s == best, n_iota, LANES), axis=1, keepdims=True)
        s = jnp.where(n_iota == idx, BELOW_NEG_INF, s)
        picks = jnp.where(n_iota == r, idx, picks)
    o_ref[0] = picks


def _select_sample(q, k_new, sums, nb):
    b, h, dh, _ = sums.shape
    d = h * dh
    return pl.pallas_call(
        functools.partial(_select_sample_kernel, nb=nb),
        grid=(b,),
        in_specs=[pl.BlockSpec((1, 1, d), lambda i: (i, 0, 0)),
                  pl.BlockSpec((1, 1, d), lambda i: (i, 0, 0)),
                  pl.BlockSpec((1, h, dh, LANES), lambda i: (i, 0, 0, 0))],
        out_specs=pl.BlockSpec((1, h, LANES), lambda i: (i, 0, 0)),
        out_shape=jax.ShapeDtypeStruct((b, h, LANES), jnp.int32),
        compiler_params=_params(1),
        name="select_sample",
    )(q, k_new, sums)


SAMPLE_HEADS_PER_STEP = 4
N_SEL_PAGES = SAMPLE_HEADS_PER_STEP * MOBA_TOPK * PAGES_PER_BLOCK


def _attn_sample_kernel(idx_ref, pt_ref, q_ref, knew_ref, vnew_ref, *refs):
    del idx_ref, pt_ref
    k_refs, v_refs, o_ref = refs[:N_SEL_PAGES], refs[N_SEL_PAGES:2 * N_SEL_PAGES], refs[-1]
    rows = 8
    width = SAMPLE_HEADS_PER_STEP * HEAD_DIM
    q = jnp.broadcast_to(q_ref[0] * (HEAD_DIM ** -0.5), (rows, width))
    k_new = jnp.broadcast_to(knew_ref[0], (rows, width))
    v_new = jnp.broadcast_to(vnew_ref[0], (rows, width))
    per_head = MOBA_TOPK * PAGES_PER_BLOCK
    heads = range(SAMPLE_HEADS_PER_STEP)
    cols = [slice(hh * HEAD_DIM, (hh + 1) * HEAD_DIM) for hh in heads]
    logits = [[_dot(q[:, cols[hh]].astype(BF16), k_refs[hh * per_head + c][0, 0].astype(BF16))
               for c in range(per_head)] for hh in heads]
    probs, denoms, accs = [], [], []
    for hh in heads:
        own = jnp.sum(q[:, cols[hh]] * k_new[:, cols[hh]], axis=1, keepdims=True)
        m = own
        for lg in logits[hh]:
            m = jnp.maximum(m, jnp.max(lg, axis=1, keepdims=True))
        p_own = jnp.exp(own - m)
        ps = [jnp.exp(lg - m) for lg in logits[hh]]
        denom = p_own
        for p in ps:
            denom = denom + jnp.sum(p, axis=1, keepdims=True)
        probs.append([p.astype(BF16) for p in ps])
        denoms.append(denom)
        accs.append(p_own * v_new[:, cols[hh]])
    outs = []
    for hh in heads:
        acc = accs[hh]
        for c in range(per_head):
            acc = acc + _dot_nt(probs[hh][c], v_refs[hh * per_head + c][0, 0].astype(BF16))
        outs.append(acc / denoms[hh])
    o_ref[0] = jnp.concatenate(outs, axis=1)[:1]


def _attn_sample(q, k_new, v_new, cache_kt, cache_vt, sel_idx, page_table, layer_page0):
    b, _, d = q.shape
    _, _, dh, page = cache_kt.shape
    n_pages = page_table.shape[0] // b
    n_groups = N_HEADS // SAMPLE_HEADS_PER_STEP
    vec_spec = pl.BlockSpec((1, 1, SAMPLE_HEADS_PER_STEP * HEAD_DIM),
                            lambda bi, hg, idx, pt: (bi, 0, hg))

    def page_spec(hh, r, j):
        def index_map(bi, hg, idx, pt):
            head = hg * SAMPLE_HEADS_PER_STEP + hh
            blk = idx[(bi * N_HEADS + head) * MOBA_TOPK + r]
            return (layer_page0 + pt[bi * n_pages + blk * PAGES_PER_BLOCK + j], head, 0, 0)
        return pl.BlockSpec((1, 1, dh, page), index_map)

    page_specs = [page_spec(hh, r, j) for hh in range(SAMPLE_HEADS_PER_STEP)
                  for r in range(MOBA_TOPK) for j in range(PAGES_PER_BLOCK)]
    return pl.pallas_call(
        _attn_sample_kernel,
        grid_spec=pltpu.PrefetchScalarGridSpec(
            num_scalar_prefetch=2,
            grid=(b, n_groups),
            in_specs=[vec_spec] * 3 + page_specs * 2,
            out_specs=vec_spec,
        ),
        out_shape=jax.ShapeDtypeStruct((b, 1, d), F32),
        compiler_params=_params(2),
        name="attn_sample",
    )(sel_idx, page_table, q, k_new, v_new,
      *([cache_kt] * N_SEL_PAGES), *([cache_vt] * N_SEL_PAGES))


FFN_TILE = 1024
GMLP_TILE = 512
PROJ_TILE = 512


def kernel(x_prompt, x_sample, cache_k, cache_v, page_table, norm_ffn1, ffn1_gate, ffn1_up, ffn1_down, norm_mix, norm_ffn2, ffn2_gate, ffn2_up, ffn2_down, gmlp_in, gmlp_vnorm_g, gmlp_vnorm_b, gmlp_ws, gmlp_bs, gmlp_out, moba_qkv, moba_out, final_norm):
    bp, s, d = x_prompt.shape
    bs, t_new, _ = x_sample.shape
    assert t_new == 1 and s % MOBA_BLOCK == 0 and d == N_HEADS * HEAD_DIM
    depth = norm_ffn1.shape[0]
    n_pool, page = cache_k.shape[1], cache_k.shape[2]
    assert page == PAGE_SIZE
    nb_prompt = s // MOBA_BLOCK

    yp = x_prompt.reshape(bp * s, d)
    ys = x_sample.reshape(bs * t_new, d)
    cache_kt = jnp.transpose(cache_k, (0, 1, 3, 4, 2)).reshape(-1, N_HEADS, HEAD_DIM, PAGE_SIZE)
    cache_vt = jnp.transpose(cache_v, (0, 1, 3, 4, 2)).reshape(-1, N_HEADS, HEAD_DIM, PAGE_SIZE)
    nb_past = page_table.shape[1] // PAGES_PER_BLOCK
    head_of_col = jnp.arange(d) // HEAD_DIM

    sv_p, sv_s, kp_l, vp_l, ks_l, vs_l = [], [], [], [], [], []
    for i in range(depth):
        last = i == depth - 1
        wg, wu, wd = ffn1_gate[i].astype(BF16), ffn1_up[i].astype(BF16), ffn1_down[i].astype(BF16)
        yp = _ffn(yp, norm_ffn1[i], wg, wu, wd, tm=FFN_TILE)
        ys = _ffn(ys, norm_ffn1[i], wg, wu, wd, tm=bs)
        j = i // 2
        if i % 2 == 0:
            w_in, w_out = gmlp_in[j].astype(BF16), gmlp_out[j].astype(BF16)
            yp, v_p = _gmlp_prompt(yp, norm_mix[i], w_in, gmlp_vnorm_g[j], gmlp_vnorm_b[j],
                                   gmlp_ws[j], gmlp_bs[j], w_out, batch=bp, tm=GMLP_TILE)
            ys, v_s = _gmlp_sample(ys, norm_mix[i], w_in, gmlp_vnorm_g[j], gmlp_vnorm_b[j],
                                   gmlp_ws[j], gmlp_bs[j], w_out)
            sv_p.append(v_p.reshape(bp, CHUNK, GMLP_GROUPS, -1))
            sv_s.append(v_s.reshape(bs, t_new, GMLP_GROUPS, -1))
        else:
            w_qkv, w_o = moba_qkv[j].astype(BF16), moba_out[j].astype(BF16)
            ktf, vtf, qt, ka, vt, means = _qkv_prompt(yp, norm_mix[i], w_qkv.T, w_qkv[:, d:2 * d],
                                                      batch=bp)
            means = means.reshape(bp, nb_prompt, d)
            means_t = jnp.where(jnp.arange(N_HEADS)[None, :, None, None] == head_of_col[None, None, None, :],
                                means[:, None, :, :], 0.0).reshape(bp, N_HEADS * nb_prompt, d).astype(BF16)
            bias = _select_prompt(qt, means_t)
            attn_p = _attn_prompt(qt, bias, ka, vt)
            yp = _proj_residual(yp, attn_p, w_o, tm=PROJ_TILE)
            kp_l.append(jnp.transpose(ktf.reshape(bp, N_HEADS, HEAD_DIM, s), (0, 3, 1, 2)))
            vp_l.append(jnp.transpose(vtf.reshape(bp, N_HEADS, HEAD_DIM, s), (0, 3, 1, 2)))
            qs, kn, vn = _qkv_sample(ys, norm_mix[i], w_qkv)
            qs3, kn3, vn3 = qs.reshape(bs, 1, d), kn.reshape(bs, 1, d), vn.reshape(bs, 1, d)
            sums = _block_key_sums(cache_kt, page_table, j * n_pool)
            picks = _select_sample(qs3, kn3, sums, nb_past)
            sel_idx = picks[:, :, :MOBA_TOPK].reshape(-1)
            attn_s = _attn_sample(qs3, kn3, vn3, cache_kt, cache_vt, sel_idx,
                                  page_table.reshape(-1), j * n_pool)
            ys = _proj_residual(ys, attn_s.reshape(bs, d), w_o, tm=bs)
            ks_l.append(kn.reshape(bs, t_new, N_HEADS, HEAD_DIM))
            vs_l.append(vn.reshape(bs, t_new, N_HEADS, HEAD_DIM))
        wg, wu, wd = ffn2_gate[i].astype(BF16), ffn2_up[i].astype(BF16), ffn2_down[i].astype(BF16)
        fg = final_norm if last else None
        yp = _ffn(yp, norm_ffn2[i], wg, wu, wd, fg, tm=FFN_TILE)
        ys = _ffn(ys, norm_ffn2[i], wg, wu, wd, fg, tm=bs)
    return (yp.reshape(bp, s, d), ys.reshape(bs, t_new, d), jnp.stack(sv_p), jnp.stack(sv_s),
            jnp.stack(kp_l), jnp.stack(vp_l), jnp.stack(ks_l), jnp.stack(vs_l))
```

```python
import functools

import jax
import jax.numpy as jnp
from jax import lax
from jax.experimental import pallas as pl
from jax.experimental.pallas import tpu as pltpu

F32 = jnp.float32
BF16 = jnp.bfloat16

EPS = 1e-6
NEG_INF = -1e30
BELOW_NEG_INF = -3e38

CHUNK = 128
GMLP_GROUPS = 8
N_HEADS = 16
HEAD_DIM = 64
MOBA_BLOCK = 256
MOBA_TOPK = 3
PAGE_SIZE = 128
PAGES_PER_BLOCK = MOBA_BLOCK // PAGE_SIZE

LANES = 128
HEADS_PER_TILE = LANES // HEAD_DIM
VMEM_LIMIT_BYTES = 56 * 1024 * 1024


def _params(n_axes):
    return pltpu.CompilerParams(dimension_semantics=("arbitrary",) * n_axes,
                                vmem_limit_bytes=VMEM_LIMIT_BYTES)


def _rms(x, g):
    return x * lax.rsqrt(jnp.mean(x * x, axis=-1, keepdims=True) + EPS) * g


def _gelu_tanh(x):
    return 0.5 * x * (1.0 + jnp.tanh(0.7978845608028654 * (x + 0.044715 * (x * x * x))))


def _dot(a, b):
    return jnp.dot(a, b, preferred_element_type=F32)


def _dot_nt(a, b):
    return lax.dot_general(a, b, (((1,), (1,)), ((), ())), preferred_element_type=F32)


def _ffn_kernel(x_ref, g_ref, wg_ref, wu_ref, wd_ref, *rest, final_norm):
    if final_norm:
        fg_ref, o_ref, h_ref, acc_ref = rest
    else:
        o_ref, h_ref, acc_ref = rest
    j = pl.program_id(1)

    @pl.when(j == 0)
    def _():
        h_ref[...] = _rms(x_ref[...], g_ref[...]).astype(BF16)
        acc_ref[...] = jnp.zeros_like(acc_ref)

    h = h_ref[...]
    gate = _dot(h, wg_ref[...])
    up = _dot(h, wu_ref[...])
    act = (gate * jax.nn.sigmoid(gate) * up).astype(BF16)
    acc_ref[...] += _dot(act, wd_ref[...])

    @pl.when(j == pl.num_programs(1) - 1)
    def _():
        y = x_ref[...] + 0.5 * acc_ref[...]
        if final_norm:
            y = _rms(y, fg_ref[...])
        o_ref[...] = y


def _ffn(x, g, wg, wu, wd, final_g=None, *, tm, tf=256):
    t, d = x.shape
    ff = wg.shape[1]
    in_specs = [
        pl.BlockSpec((tm, d), lambda i, j: (i, 0)),
        pl.BlockSpec((1, d), lambda i, j: (0, 0)),
        pl.BlockSpec((d, tf), lambda i, j: (0, j)),
        pl.BlockSpec((d, tf), lambda i, j: (0, j)),
        pl.BlockSpec((tf, d), lambda i, j: (j, 0)),
    ]
    args = [x, g.reshape(1, d), wg, wu, wd]
    if final_g is not None:
        in_specs.append(pl.BlockSpec((1, d), lambda i, j: (0, 0)))
        args.append(final_g.reshape(1, d))
    return pl.pallas_call(
        functools.partial(_ffn_kernel, final_norm=final_g is not None),
        grid=(t // tm, ff // tf),
        in_specs=in_specs,
        out_specs=pl.BlockSpec((tm, d), lambda i, j: (i, 0)),
        out_shape=jax.ShapeDtypeStruct((t, d), F32),
        scratch_shapes=[pltpu.VMEM((tm, d), BF16), pltpu.VMEM((tm, d), F32)],
        compiler_params=_params(2),
        name="ffn",
    )(*args)


def _group_layernorm(v, gain, bias):
    mu = jnp.mean(v, axis=-1, keepdims=True)
    d = v - mu
    var = jnp.mean(d * d, axis=-1, keepdims=True)
    return d * lax.rsqrt(var + EPS) * gain + bias


def _gmlp_prompt_kernel(x_ref, g_ref, win_ref, vg_ref, vb_ref, ws_ref, bs_ref, wout_ref,
                        o_ref, sv_ref):
    tm, d = x_ref.shape
    width = win_ref.shape[1] // 2
    gd = width // GMLP_GROUPS
    n_chunks = tm // CHUNK
    x = x_ref[...]
    h = _rms(x, g_ref[...]).astype(BF16)
    z = _gelu_tanh(_dot(h, win_ref[...]))
    row = lax.broadcasted_iota(jnp.int32, (CHUNK, CHUNK), 0)
    col = lax.broadcasted_iota(jnp.int32, (CHUNK, CHUNK), 1)
    is_last = pl.program_id(1) == pl.num_programs(1) - 1
    gated = []
    for grp in range(GMLP_GROUPS):
        lanes = slice(grp * gd, (grp + 1) * gd)
        vn = _group_layernorm(z[:, width + grp * gd: width + (grp + 1) * gd],
                              vg_ref[:, lanes], vb_ref[:, lanes])

        @pl.when(is_last)
        def _(vn=vn, lanes=lanes):
            sv_ref[0, :, lanes] = vn[tm - CHUNK:, :]

        ws = jnp.where(row >= col, ws_ref[grp], 0.0).astype(BF16)
        rhs = jnp.concatenate([vn[c * CHUNK:(c + 1) * CHUNK] for c in range(n_chunks)],
                              axis=1).astype(BF16)
        mixed_cat = _dot(ws, rhs)
        mixed = jnp.concatenate(
            [mixed_cat[:, c * gd:(c + 1) * gd] + bs_ref[grp] for c in range(n_chunks)], axis=0)
        gated.append((z[:, lanes] * mixed).astype(BF16))
    a = jnp.concatenate(gated, axis=1)
    o_ref[...] = x + _dot(a, wout_ref[...])


def _gmlp_prompt(x, g, w_in, vn_g, vn_b, w_s, b_s, w_out, *, batch, tm):
    t, d = x.shape
    width = w_out.shape[0]
    gd = width // GMLP_GROUPS
    tiles = t // batch // tm
    bs_b = jnp.broadcast_to(b_s[:, :, None], (GMLP_GROUPS, CHUNK, gd))
    const2 = lambda b, i: (0, 0)
    const3 = lambda b, i: (0, 0, 0)
    return pl.pallas_call(
        _gmlp_prompt_kernel,
        grid=(batch, tiles),
        in_specs=[
            pl.BlockSpec((tm, d), lambda b, i: (b * tiles + i, 0)),
            pl.BlockSpec((1, d), const2),
            pl.BlockSpec((d, 2 * width), const2),
            pl.BlockSpec((1, width), const2),
            pl.BlockSpec((1, width), const2),
            pl.BlockSpec((GMLP_GROUPS, CHUNK, CHUNK), const3),
            pl.BlockSpec((GMLP_GROUPS, CHUNK, gd), const3),
            pl.BlockSpec((width, d), const2),
        ],
        out_specs=[
            pl.BlockSpec((tm, d), lambda b, i: (b * tiles + i, 0)),
            pl.BlockSpec((1, CHUNK, width), lambda b, i: (b, 0, 0)),
        ],
        out_shape=[jax.ShapeDtypeStruct((t, d), F32),
                   jax.ShapeDtypeStruct((batch, CHUNK, width), F32)],
        compiler_params=_params(2),
        name="gmlp_prompt",
    )(x, g.reshape(1, d), w_in, vn_g.reshape(1, width), vn_b.reshape(1, width), w_s, bs_b, w_out)


def _gmlp_sample_kernel(x_ref, g_ref, win_ref, vg_ref, vb_ref, coef_ref, bias_ref, wout_ref,
                        o_ref, sv_ref):
    width = win_ref.shape[1] // 2
    gd = width // GMLP_GROUPS
    x = x_ref[...]
    h = _rms(x, g_ref[...]).astype(BF16)
    z = _gelu_tanh(_dot(h, win_ref[...]))
    vn = jnp.concatenate(
        [_group_layernorm(z[:, width + grp * gd: width + (grp + 1) * gd],
                          vg_ref[:, grp * gd:(grp + 1) * gd], vb_ref[:, grp * gd:(grp + 1) * gd])
         for grp in range(GMLP_GROUPS)], axis=1)
    sv_ref[...] = vn
    mixed = vn * coef_ref[...] + bias_ref[...]
    o_ref[...] = x + _dot((z[:, :width] * mixed).astype(BF16), wout_ref[...])


def _gmlp_sample(x, g, w_in, vn_g, vn_b, w_s, b_s, w_out):
    t, d = x.shape
    width = w_out.shape[0]
    gd = width // GMLP_GROUPS
    coef = jnp.repeat(w_s[:, 0, 0], gd).reshape(1, width)
    bias = jnp.repeat(b_s[:, 0], gd).reshape(1, width)
    return pl.pallas_call(
        _gmlp_sample_kernel,
        out_shape=[jax.ShapeDtypeStruct((t, d), F32), jax.ShapeDtypeStruct((t, width), F32)],
        compiler_params=pltpu.CompilerParams(vmem_limit_bytes=VMEM_LIMIT_BYTES),
        name="gmlp_sample",
    )(x, g.reshape(1, d), w_in, vn_g.reshape(1, width), vn_b.reshape(1, width), coef, bias, w_out)


LOG2_E = 1.4426950408889634
Q_SCALE_LOG2 = HEAD_DIM ** -0.5 * LOG2_E


def _qkv_prompt_kernel(x_ref, g_ref, wt_ref, wk_ref, ktf_ref, vtf_ref, qtb_ref, kab_ref, vtb_ref,
                       mean_ref):
    tm, d = x_ref.shape
    n_pairs = d // LANES
    h = _rms(x_ref[...], g_ref[...]).astype(BF16)
    qkv_t = _dot_nt(wt_ref[...], h)
    ktf_ref[0] = qkv_t[d:2 * d]
    vtf_ref[0] = qkv_t[2 * d:]
    qtb_ref[0, 0] = (qkv_t[:d] * Q_SCALE_LOG2).astype(BF16)
    vtb_ref[0, 0] = qkv_t[2 * d:].astype(BF16)
    k = _dot(h, wk_ref[...])
    mean_ref[0] = jnp.sum(k, axis=0, keepdims=True) * (1.0 / MOBA_BLOCK)
    kb = k.astype(BF16)
    lane = lax.broadcasted_iota(jnp.int32, (tm, LANES), 1)
    onehot = jnp.where(lane == pl.program_id(1), 1.0, 0.0).astype(BF16)
    pieces = []
    for hp in range(n_pairs):
        pieces += [kb[:, hp * LANES:(hp + 1) * LANES], onehot]
    kab_ref[...] = jnp.concatenate(pieces, axis=1)


def _qkv_prompt(x, g, w_t, w_k, *, batch):
    t, d = x.shape
    tm = MOBA_BLOCK
    nb = t // batch // tm
    assert nb <= LANES
    const = lambda b, i: (0, 0)
    feat_spec = pl.BlockSpec((1, d, tm), lambda b, i: (b, 0, i))
    blk_spec = pl.BlockSpec((1, 1, d, tm), lambda b, i: (b, i, 0, 0))
    return pl.pallas_call(
        _qkv_prompt_kernel,
        grid=(batch, nb),
        in_specs=[pl.BlockSpec((tm, d), lambda b, i: (b * nb + i, 0)), pl.BlockSpec((1, d), const),
                  pl.BlockSpec((3 * d, d), const), pl.BlockSpec((d, d), const)],
        out_specs=[feat_spec, feat_spec, blk_spec,
                   pl.BlockSpec((tm, 2 * d), lambda b, i: (b * nb + i, 0)), blk_spec,
                   pl.BlockSpec((1, 1, d), lambda b, i: (b * nb + i, 0, 0))],
        out_shape=[jax.ShapeDtypeStruct((batch, d, nb * tm), F32)] * 2
        + [jax.ShapeDtypeStruct((batch, nb, d, tm), BF16), jax.ShapeDtypeStruct((t, 2 * d), BF16),
           jax.ShapeDtypeStruct((batch, nb, d, tm), BF16), jax.ShapeDtypeStruct((batch * nb, 1, d), F32)],
        compiler_params=_params(2),
        name="qkv_prompt",
    )(x, g.reshape(1, d), w_t, w_k)


def _qkv_sample_kernel(x_ref, g_ref, w_ref, q_ref, k_ref, v_ref):
    d = x_ref.shape[1]
    h = _rms(x_ref[...], g_ref[...]).astype(BF16)
    qkv = _dot(h, w_ref[...])
    q_ref[...] = qkv[:, :d]
    k_ref[...] = qkv[:, d:2 * d]
    v_ref[...] = qkv[:, 2 * d:]


def _qkv_sample(x, g, w):
    t, d = x.shape
    return pl.pallas_call(
        _qkv_sample_kernel,
        out_shape=[jax.ShapeDtypeStruct((t, d), F32)] * 3,
        compiler_params=pltpu.CompilerParams(vmem_limit_bytes=VMEM_LIMIT_BYTES),
        name="qkv_sample",
    )(x, g.reshape(1, d), w)


def _proj_kernel(x_ref, a_ref, w_ref, o_ref):
    o_ref[...] = x_ref[...] + _dot(a_ref[...].astype(BF16), w_ref[...])


def _proj_residual(x, a, w, *, tm):
    t, d = x.shape
    row_spec = pl.BlockSpec((tm, d), lambda i: (i, 0))
    return pl.pallas_call(
        _proj_kernel,
        grid=(t // tm,),
        in_specs=[row_spec, row_spec, pl.BlockSpec((d, d), lambda i: (0, 0))],
        out_specs=row_spec,
        out_shape=jax.ShapeDtypeStruct((t, d), F32),
        compiler_params=_params(1),
        name="proj_residual",
    )(x, a, w)


def _top_k_mask(scores, n_iota, n_valid):
    axis = scores.ndim - 2
    n_total = scores.shape[axis]
    s = jnp.where(n_iota < n_valid, scores, NEG_INF)
    picked = jnp.zeros(scores.shape, F32)
    idxs = []
    for _ in range(MOBA_TOPK):
        best = jnp.max(s, axis=axis, keepdims=True)
        idx = jnp.min(jnp.where(s == best, n_iota, n_total), axis=axis, keepdims=True)
        hit = n_iota == idx
        picked = jnp.where(hit, 1.0, picked)
        s = jnp.where(hit, BELOW_NEG_INF, s)
        idxs.append(idx)
    return picked, idxs


def _select_prompt_kernel(qt_ref, mt_ref, o_ref):
    tq = qt_ref.shape[3]
    nb = mt_ref.shape[1] // N_HEADS
    own = pl.program_id(1)
    st = _dot(mt_ref[0], qt_ref[0, 0])
    st = st.reshape(N_HEADS, nb, tq)
    n_iota = lax.broadcasted_iota(jnp.int32, st.shape, 1)
    picked, _ = _top_k_mask(st, n_iota, own)
    bias = jnp.where(n_iota < own, jnp.where(picked > 0.0, 0.0, NEG_INF), NEG_INF)
    o_ref[0] = bias.astype(BF16)


def _select_prompt(qt, means_t):
    batch, nb, d, tq = qt.shape
    return pl.pallas_call(
        _select_prompt_kernel,
        grid=(batch, nb),
        in_specs=[pl.BlockSpec((1, 1, d, tq), lambda b, i: (b, i, 0, 0)),
                  pl.BlockSpec((1, N_HEADS * nb, d), lambda b, i: (b, 0, 0))],
        out_specs=pl.BlockSpec((1, N_HEADS, nb, tq), lambda b, i: (b, 0, 0, i)),
        out_shape=jax.ShapeDtypeStruct((batch, N_HEADS, nb, nb * tq), BF16),
        compiler_params=_params(2),
        name="select_prompt",
    )(qt, means_t)


PAIRS_PER_STEP = 2
BLOCKS_PER_TRIP = 4
SCORE_LEAD = 8


PAGES_PER_STEP = 16
SUM_ROWS = 16


def _accumulate_block_key_sums(page_refs, sums_ref, part):
    blocks_per_step = PAGES_PER_STEP // PAGES_PER_BLOCK

    @pl.when(part == 0)
    def _():
        sums_ref[...] = jnp.zeros_like(sums_ref)

    lane = lax.broadcasted_iota(jnp.int32, sums_ref.shape[1:], 2)
    acc = sums_ref[0]
    for blk in range(blocks_per_step):
        tot = page_refs[blk * PAGES_PER_BLOCK][0]
        for j in range(1, PAGES_PER_BLOCK):
            tot = tot + page_refs[blk * PAGES_PER_BLOCK + j][0]
        col = jnp.sum(tot, axis=-1, keepdims=True)
        acc = jnp.where(lane == part * blocks_per_step + blk, col, acc)
    sums_ref[0] = acc


def _attn_prompt_kernel(pt_ref, qt_ref, bias_ref, ka_ref, vt_ref, *rest, steps_per_seq):
    del pt_ref
    page_refs = rest[:PAGES_PER_STEP]
    o_ref, sums_ref, qa_ref = rest[PAGES_PER_STEP:]
    step = (pl.program_id(0) * pl.num_programs(1) + pl.program_id(1)) * pl.num_programs(2) \
        + pl.program_id(2)
    _accumulate_block_key_sums(page_refs, sums_ref, lax.rem(step, steps_per_seq))

    tq = qt_ref.shape[3]
    nb = bias_ref.shape[2]
    qi = pl.program_id(2)
    n_chains = PAIRS_PER_STEP * HEADS_PER_TILE
    ones_rows = jnp.ones((SUM_ROWS, MOBA_BLOCK), BF16)
    feat = lax.broadcasted_iota(jnp.int32, (LANES, tq), 0)
    zeros_bias_pad = jnp.zeros((LANES - nb, tq), BF16)
    for c in range(n_chains):
        p, hh = divmod(c, HEADS_PER_TILE)
        qt = qt_ref[0, 0, p * LANES:(p + 1) * LANES, :]
        in_head = (feat >= hh * HEAD_DIM) & (feat < (hh + 1) * HEAD_DIM)
        qa_ref[c, :LANES, :] = jnp.where(in_head, qt, jnp.zeros_like(qt))
        qa_ref[c, LANES:, :] = jnp.concatenate([bias_ref[0, c], zeros_bias_pad], axis=0)

    def v_rows(c, blk):
        p, hh = divmod(c, HEADS_PER_TILE)
        r0 = p * LANES + hh * HEAD_DIM
        return jnp.concatenate([vt_ref[0, blk, r0:r0 + HEAD_DIM, :], ones_rows], axis=0)


    start = pl.multiple_of(qi * tq, tq)
    key_pos = lax.broadcasted_iota(jnp.int32, (tq, tq), 0)
    qry_pos = lax.broadcasted_iota(jnp.int32, (tq, tq), 1)
    scores = []
    for c in range(n_chains):
        p = c // HEADS_PER_TILE
        kd = ka_ref[pl.ds(start, tq), p * 2 * LANES:p * 2 * LANES + LANES]
        scores.append(_dot(kd, qa_ref[c, :LANES, :]))
    maxes, probs = [], []
    for c in range(n_chains):
        s = jnp.where(key_pos <= qry_pos, scores[c], NEG_INF).astype(BF16)
        m = jnp.max(s, axis=0, keepdims=True)
        maxes.append(m)
        probs.append(jnp.exp2(s - m))
    state = [x for c in range(n_chains) for x in (maxes[c], _dot(v_rows(c, qi), probs[c]))]

    def body(t, state):
        blocks = [t * BLOCKS_PER_TRIP + j for j in range(BLOCKS_PER_TRIP)]

        def score_product(c, n):
            p = c // HEADS_PER_TILE
            off = pl.multiple_of(n * MOBA_BLOCK, MOBA_BLOCK)
            ka = ka_ref[pl.ds(off, MOBA_BLOCK), p * 2 * LANES:(p + 1) * 2 * LANES]
            return _dot(ka, qa_ref[c]).astype(BF16)

        def online_softmax_update(m, acc, s, v_aug):
            m_new = jnp.maximum(m, jnp.max(s, axis=0, keepdims=True))
            acc = jnp.exp2((m - m_new).astype(F32)) * acc + _dot(v_aug, jnp.exp2(s - m_new))
            return m_new, acc

        units = [(c, n) for n in blocks for c in range(n_chains)]
        chain_state = {c: (state[2 * c], state[2 * c + 1]) for c in range(n_chains)}
        scores = {}
        for k in range(len(units) + SCORE_LEAD):
            if k < len(units):
                scores[k] = score_product(*units[k])
            if k >= SCORE_LEAD:
                c, n = units[k - SCORE_LEAD]
                chain_state[c] = online_softmax_update(*chain_state[c], scores.pop(k - SCORE_LEAD),
                                                       v_rows(c, n))
        return tuple(x for c in range(n_chains) for x in chain_state[c])

    n_trips = (qi + BLOCKS_PER_TRIP - 1) // BLOCKS_PER_TRIP
    state = lax.fori_loop(0, n_trips, body, tuple(state))
    outs = []
    for c in range(n_chains):
        acc = state[2 * c + 1]
        outs.append((acc[:HEAD_DIM] / acc[HEAD_DIM:HEAD_DIM + 1]).T)
    o_ref[...] = jnp.concatenate(outs, axis=1).astype(o_ref.dtype)


def _attn_prompt(qt, bias, ka, vt, cache_kt, page_table, layer_page0):
    batch, nb, d, tq = qt.shape
    s = nb * tq
    n_groups = d // (PAIRS_PER_STEP * LANES)
    gw = PAIRS_PER_STEP * LANES
    n_chains = PAIRS_PER_STEP * HEADS_PER_TILE
    bs, n_pages = page_table.shape
    _, h, dh, page = cache_kt.shape
    steps_per_seq = n_pages // PAGES_PER_STEP
    assert n_pages % PAGES_PER_STEP == 0 and n_pages // PAGES_PER_BLOCK < LANES
    assert batch * n_groups * nb == bs * steps_per_seq, "cache pages must tile the attention grid"

    def step_of(b, g, i):
        return (b * n_groups + g) * nb + i

    def page_spec(j):
        def index_map(b, g, i, pt):
            step = step_of(b, g, i)
            page_no = lax.rem(step, steps_per_seq) * PAGES_PER_STEP + j
            return (layer_page0 + pt[lax.div(step, steps_per_seq), page_no], 0, 0, 0)
        return pl.BlockSpec((1, h, dh, page), index_map)

    return pl.pallas_call(
        functools.partial(_attn_prompt_kernel, steps_per_seq=steps_per_seq),
        grid_spec=pltpu.PrefetchScalarGridSpec(
            num_scalar_prefetch=1,
            grid=(batch, n_groups, nb),
            in_specs=[pl.BlockSpec((1, 1, gw, tq), lambda b, g, i, pt: (b, i, g, 0)),
                      pl.BlockSpec((1, n_chains, nb, tq), lambda b, g, i, pt: (b, g, 0, i)),
                      pl.BlockSpec((s, 2 * gw), lambda b, g, i, pt: (b, g)),
                      pl.BlockSpec((1, nb, gw, tq), lambda b, g, i, pt: (b, 0, g, 0))]
            + [page_spec(j) for j in range(PAGES_PER_STEP)],
            out_specs=[pl.BlockSpec((tq, gw), lambda b, g, i, pt: (b * nb + i, g)),
                       pl.BlockSpec((1, h, dh, LANES),
                                    lambda b, g, i, pt: (lax.div(step_of(b, g, i), steps_per_seq), 0, 0, 0))],
            scratch_shapes=[pltpu.VMEM((n_chains, 2 * LANES, tq), BF16)],
        ),
        out_shape=[jax.ShapeDtypeStruct((batch * s, d), BF16),
                   jax.ShapeDtypeStruct((bs, h, dh, LANES), F32)],
        compiler_params=_params(3),
        name="attn_prompt",
    )(page_table, qt, bias, ka, vt, *([cache_kt] * PAGES_PER_STEP))


def _as_column(row):
    d = row.shape[1]
    return jnp.concatenate(
        [jnp.broadcast_to(row[:, c * LANES:(c + 1) * LANES], (LANES, LANES)).T
         for c in range(d // LANES)], axis=0)


def _select_sample_kernel(q_ref, knew_ref, sums_ref, o_ref, *, nb):
    _, h, dh, _ = sums_ref.shape
    lane3 = lax.broadcasted_iota(jnp.int32, (h, dh, LANES), 2)
    q_col = _as_column(q_ref[0]).reshape(h, dh, LANES)
    sums = jnp.where(lane3 == nb, _as_column(knew_ref[0]).reshape(h, dh, LANES), sums_ref[0])
    scores = jnp.sum(sums * (1.0 / MOBA_BLOCK) * q_col, axis=1)
    n_iota = lax.broadcasted_iota(jnp.int32, scores.shape, 1)
    s = jnp.where(n_iota < nb, scores, NEG_INF)
    picks = jnp.zeros(scores.shape, jnp.int32)
    for r in range(MOBA_TOPK):
        best = jnp.max(s, axis=1, keepdims=True)
        idx = jnp.min(jnp.where(s == best, n_iota, LANES), axis=1, keepdims=True)
        s = jnp.where(n_iota == idx, BELOW_NEG_INF, s)
        picks = jnp.where(n_iota == r, idx, picks)
    o_ref[0] = picks


def _select_sample(q, k_new, sums, nb):
    b, h, dh, _ = sums.shape
    d = h * dh
    return pl.pallas_call(
        functools.partial(_select_sample_kernel, nb=nb),
        grid=(b,),
        in_specs=[pl.BlockSpec((1, 1, d), lambda i: (i, 0, 0)),
                  pl.BlockSpec((1, 1, d), lambda i: (i, 0, 0)),
                  pl.BlockSpec((1, h, dh, LANES), lambda i: (i, 0, 0, 0))],
        out_specs=pl.BlockSpec((1, h, LANES), lambda i: (i, 0, 0)),
        out_shape=jax.ShapeDtypeStruct((b, h, LANES), jnp.int32),
        compiler_params=_params(1),
        name="select_sample",
    )(q, k_new, sums)


SAMPLE_HEADS_PER_STEP = 4
N_SEL_PAGES = SAMPLE_HEADS_PER_STEP * MOBA_TOPK * PAGES_PER_BLOCK


def _attn_sample_kernel(idx_ref, pt_ref, q_ref, knew_ref, vnew_ref, *refs):
    del idx_ref, pt_ref
    k_refs, v_refs, o_ref = refs[:N_SEL_PAGES], refs[N_SEL_PAGES:2 * N_SEL_PAGES], refs[-1]
    rows = 8
    width = SAMPLE_HEADS_PER_STEP * HEAD_DIM
    q = jnp.broadcast_to(q_ref[0] * (HEAD_DIM ** -0.5), (rows, width))
    k_new = jnp.broadcast_to(knew_ref[0], (rows, width))
    v_new = jnp.broadcast_to(vnew_ref[0], (rows, width))
    per_head = MOBA_TOPK * PAGES_PER_BLOCK
    heads = range(SAMPLE_HEADS_PER_STEP)
    cols = [slice(hh * HEAD_DIM, (hh + 1) * HEAD_DIM) for hh in heads]
    logits = [[_dot(q[:, cols[hh]].astype(BF16), k_refs[hh * per_head + c][0, 0].astype(BF16))
               for c in range(per_head)] for hh in heads]
    probs, denoms, accs = [], [], []
    for hh in heads:
        own = jnp.sum(q[:, cols[hh]] * k_new[:, cols[hh]], axis=1, keepdims=True)
        m = own
        for lg in logits[hh]:
            m = jnp.maximum(m, jnp.max(lg, axis=1, keepdims=True))
        p_own = jnp.exp(own - m)
        ps = [jnp.exp(lg - m) for lg in logits[hh]]
        denom = p_own
        for p in ps:
            denom = denom + jnp.sum(p, axis=1, keepdims=True)
        probs.append([p.astype(BF16) for p in ps])
        denoms.append(denom)
        accs.append(p_own * v_new[:, cols[hh]])
    outs = []
    for hh in heads:
        acc = accs[hh]
        for c in range(per_head):
            acc = acc + _dot_nt(probs[hh][c], v_refs[hh * per_head + c][0, 0].astype(BF16))
        outs.append(acc / denoms[hh])
    o_ref[0] = jnp.concatenate(outs, axis=1)[:1]


def _attn_sample(q, k_new, v_new, cache_kt, cache_vt, sel_idx, page_table, layer_page0):
    b, _, d = q.shape
    _, _, dh, page = cache_kt.shape
    n_pages = page_table.shape[0] // b
    n_groups = N_HEADS // SAMPLE_HEADS_PER_STEP
    vec_spec = pl.BlockSpec((1, 1, SAMPLE_HEADS_PER_STEP * HEAD_DIM),
                            lambda bi, hg, idx, pt: (bi, 0, hg))

    def page_spec(hh, r, j):
        def index_map(bi, hg, idx, pt):
            head = hg * SAMPLE_HEADS_PER_STEP + hh
            blk = idx[(bi * N_HEADS + head) * MOBA_TOPK + r]
            return (layer_page0 + pt[bi * n_pages + blk * PAGES_PER_BLOCK + j], head, 0, 0)
        return pl.BlockSpec((1, 1, dh, page), index_map)

    page_specs = [page_spec(hh, r, j) for hh in range(SAMPLE_HEADS_PER_STEP)
                  for r in range(MOBA_TOPK) for j in range(PAGES_PER_BLOCK)]
    return pl.pallas_call(
        _attn_sample_kernel,
        grid_spec=pltpu.PrefetchScalarGridSpec(
            num_scalar_prefetch=2,
            grid=(b, n_groups),
            in_specs=[vec_spec] * 3 + page_specs * 2,
            out_specs=vec_spec,
        ),
        out_shape=jax.ShapeDtypeStruct((b, 1, d), F32),
        compiler_params=_params(2),
        name="attn_sample",
    )(sel_idx, page_table, q, k_new, v_new,
      *([cache_kt] * N_SEL_PAGES), *([cache_vt] * N_SEL_PAGES))


FFN_TILE = 1024
GMLP_TILE = 512
PROJ_TILE = 512


def kernel(x_prompt, x_sample, cache_k, cache_v, page_table, norm_ffn1, ffn1_gate, ffn1_up, ffn1_down, norm_mix, norm_ffn2, ffn2_gate, ffn2_up, ffn2_down, gmlp_in, gmlp_vnorm_g, gmlp_vnorm_b, gmlp_ws, gmlp_bs, gmlp_out, moba_qkv, moba_out, final_norm):
    bp, s, d = x_prompt.shape
    bs, t_new, _ = x_sample.shape
    assert t_new == 1 and s % MOBA_BLOCK == 0 and d == N_HEADS * HEAD_DIM
    depth = norm_ffn1.shape[0]
    n_pool, page = cache_k.shape[1], cache_k.shape[2]
    assert page == PAGE_SIZE
    nb_prompt = s // MOBA_BLOCK

    yp = x_prompt.reshape(bp * s, d)
    ys = x_sample.reshape(bs * t_new, d)
    cache_kt = jnp.transpose(cache_k, (0, 1, 3, 4, 2)).reshape(-1, N_HEADS, HEAD_DIM, PAGE_SIZE)
    cache_vt = jnp.transpose(cache_v, (0, 1, 3, 4, 2)).reshape(-1, N_HEADS, HEAD_DIM, PAGE_SIZE)
    nb_past = page_table.shape[1] // PAGES_PER_BLOCK
    head_of_col = jnp.arange(d) // HEAD_DIM

    sv_p, sv_s, kp_l, vp_l, ks_l, vs_l = [], [], [], [], [], []
    for i in range(depth):
        last = i == depth - 1
        wg, wu, wd = ffn1_gate[i].astype(BF16), ffn1_up[i].astype(BF16), ffn1_down[i].astype(BF16)
        yp = _ffn(yp, norm_ffn1[i], wg, wu, wd, tm=FFN_TILE)
        ys = _ffn(ys, norm_ffn1[i], wg, wu, wd, tm=bs)
        j = i // 2
        if i % 2 == 0:
            w_in, w_out = gmlp_in[j].astype(BF16), gmlp_out[j].astype(BF16)
            yp, v_p = _gmlp_prompt(yp, norm_mix[i], w_in, gmlp_vnorm_g[j], gmlp_vnorm_b[j],
                                   gmlp_ws[j], gmlp_bs[j], w_out, batch=bp, tm=GMLP_TILE)
            ys, v_s = _gmlp_sample(ys, norm_mix[i], w_in, gmlp_vnorm_g[j], gmlp_vnorm_b[j],
                                   gmlp_ws[j], gmlp_bs[j], w_out)
            sv_p.append(v_p.reshape(bp, CHUNK, GMLP_GROUPS, -1))
            sv_s.append(v_s.reshape(bs, t_new, GMLP_GROUPS, -1))
        else:
            w_qkv, w_o = moba_qkv[j].astype(BF16), moba_out[j].astype(BF16)
            ktf, vtf, qt, ka, vt, means = _qkv_prompt(yp, norm_mix[i], w_qkv.T, w_qkv[:, d:2 * d],
                                                      batch=bp)
            means = means.reshape(bp, nb_prompt, d)
            means_t = jnp.where(jnp.arange(N_HEADS)[None, :, None, None] == head_of_col[None, None, None, :],
                                means[:, None, :, :], 0.0).reshape(bp, N_HEADS * nb_prompt, d).astype(BF16)
            bias = _select_prompt(qt, means_t)
            attn_p, sums = _attn_prompt(qt, bias, ka, vt, cache_kt, page_table, j * n_pool)
            yp = _proj_residual(yp, attn_p, w_o, tm=PROJ_TILE)
            kp_l.append(jnp.transpose(ktf.reshape(bp, N_HEADS, HEAD_DIM, s), (0, 3, 1, 2)))
            vp_l.append(jnp.transpose(vtf.reshape(bp, N_HEADS, HEAD_DIM, s), (0, 3, 1, 2)))
            qs, kn, vn = _qkv_sample(ys, norm_mix[i], w_qkv)
            qs3, kn3, vn3 = qs.reshape(bs, 1, d), kn.reshape(bs, 1, d), vn.reshape(bs, 1, d)
            picks = _select_sample(qs3, kn3, sums, nb_past)
            sel_idx = picks[:, :, :MOBA_TOPK].reshape(-1)
            attn_s = _attn_sample(qs3, kn3, vn3, cache_kt, cache_vt, sel_idx,
                                  page_table.reshape(-1), j * n_pool)
            ys = _proj_residual(ys, attn_s.reshape(bs, d), w_o, tm=bs)
            ks_l.append(kn.reshape(bs, t_new, N_HEADS, HEAD_DIM))
            vs_l.append(vn.reshape(bs, t_new, N_HEADS, HEAD_DIM))
        wg, wu, wd = ffn2_gate[i].astype(BF16), ffn2_up[i].astype(BF16), ffn2_down[i].astype(BF16)
        fg = final_norm if last else None
        yp = _ffn(yp, norm_ffn2[i], wg, wu, wd, fg, tm=FFN_TILE)
        ys = _ffn(ys, norm_ffn2[i], wg, wu, wd, fg, tm=bs)
    return (yp.reshape(bp, s, d), ys.reshape(bs, t_new, d), jnp.stack(sv_p), jnp.stack(sv_s),
            jnp.stack(kp_l), jnp.stack(vp_l), jnp.stack(ks_l), jnp.stack(vs_l))
```

```python
import functools

import jax
import jax.numpy as jnp
from jax import lax
from jax.experimental import pallas as pl
from jax.experimental.pallas import tpu as pltpu

F32 = jnp.float32
BF16 = jnp.bfloat16

EPS = 1e-6
NEG_INF = -1e30
BELOW_NEG_INF = -3e38

CHUNK = 128
GMLP_GROUPS = 8
N_HEADS = 16
HEAD_DIM = 64
MOBA_BLOCK = 256
MOBA_TOPK = 3
PAGE_SIZE = 128
PAGES_PER_BLOCK = MOBA_BLOCK // PAGE_SIZE

LANES = 128
HEADS_PER_TILE = LANES // HEAD_DIM
VMEM_LIMIT_BYTES = 56 * 1024 * 1024


def _params(n_axes):
    return pltpu.CompilerParams(dimension_semantics=("arbitrary",) * n_axes,
                                vmem_limit_bytes=VMEM_LIMIT_BYTES)


def _rms(x, g):
    return x * lax.rsqrt(jnp.mean(x * x, axis=-1, keepdims=True) + EPS) * g


def _gelu_tanh(x):
    return 0.5 * x * (1.0 + jnp.tanh(0.7978845608028654 * (x + 0.044715 * (x * x * x))))


def _div(x, n):
    if n & (n - 1) == 0:
        return lax.shift_right_logical(x, n.bit_length() - 1)
    return lax.div(x, n)


def _rem(x, n):
    if n & (n - 1) == 0:
        return x & (n - 1)
    return lax.rem(x, n)


def _dot(a, b):
    return jnp.dot(a, b, preferred_element_type=F32)


def _dot_nt(a, b):
    return lax.dot_general(a, b, (((1,), (1,)), ((), ())), preferred_element_type=F32)


def _ffn_kernel(x_ref, g_ref, wg_ref, wu_ref, wd_ref, *rest, final_norm):
    if final_norm:
        fg_ref, o_ref, h_ref, acc_ref = rest
    else:
        o_ref, h_ref, acc_ref = rest
    j = pl.program_id(1)

    @pl.when(j == 0)
    def _():
        h_ref[...] = _rms(x_ref[...], g_ref[...]).astype(BF16)
        acc_ref[...] = jnp.zeros_like(acc_ref)

    h = h_ref[...]
    gate = _dot(h, wg_ref[...].astype(BF16))
    up = _dot(h, wu_ref[...].astype(BF16))
    act = (gate * jax.nn.sigmoid(gate) * up).astype(BF16)
    acc_ref[...] += _dot(act, wd_ref[...].astype(BF16))

    @pl.when(j == pl.num_programs(1) - 1)
    def _():
        y = x_ref[...] + 0.5 * acc_ref[...]
        if final_norm:
            y = _rms(y, fg_ref[...])
        o_ref[...] = y


def _ffn(x, g, wg, wu, wd, layer, final_g=None, *, tm, tf=256):
    t, d = x.shape
    ff = wg.shape[2]
    in_specs = [
        pl.BlockSpec((tm, d), lambda i, j: (i, 0)),
        pl.BlockSpec((1, d), lambda i, j: (0, 0)),
        pl.BlockSpec((None, d, tf), lambda i, j: (layer, 0, j)),
        pl.BlockSpec((None, d, tf), lambda i, j: (layer, 0, j)),
        pl.BlockSpec((None, tf, d), lambda i, j: (layer, j, 0)),
    ]
    args = [x, g.reshape(1, d), wg, wu, wd]
    if final_g is not None:
        in_specs.append(pl.BlockSpec((1, d), lambda i, j: (0, 0)))
        args.append(final_g.reshape(1, d))
    return pl.pallas_call(
        functools.partial(_ffn_kernel, final_norm=final_g is not None),
        grid=(t // tm, ff // tf),
        in_specs=in_specs,
        out_specs=pl.BlockSpec((tm, d), lambda i, j: (i, 0)),
        out_shape=jax.ShapeDtypeStruct((t, d), F32),
        scratch_shapes=[pltpu.VMEM((tm, d), BF16), pltpu.VMEM((tm, d), F32)],
        compiler_params=_params(2),
        name="ffn",
    )(*args)


def _group_layernorm(v, gain, bias):
    mu = jnp.mean(v, axis=-1, keepdims=True)
    d = v - mu
    var = jnp.mean(d * d, axis=-1, keepdims=True)
    return d * lax.rsqrt(var + EPS) * gain + bias


def _gmlp_prompt_kernel(x_ref, g_ref, win_ref, vg_ref, vb_ref, ws_ref, bs_ref, wout_ref,
                        o_ref, sv_ref):
    tm, d = x_ref.shape
    width = win_ref.shape[1] // 2
    gd = width // GMLP_GROUPS
    n_chunks = tm // CHUNK
    x = x_ref[...]
    h = _rms(x, g_ref[...]).astype(BF16)
    z = _gelu_tanh(_dot(h, win_ref[...]))
    row = lax.broadcasted_iota(jnp.int32, (CHUNK, CHUNK), 0)
    col = lax.broadcasted_iota(jnp.int32, (CHUNK, CHUNK), 1)
    is_last = pl.program_id(1) == pl.num_programs(1) - 1
    gated = []
    for grp in range(GMLP_GROUPS):
        lanes = slice(grp * gd, (grp + 1) * gd)
        vn = _group_layernorm(z[:, width + grp * gd: width + (grp + 1) * gd],
                              vg_ref[:, lanes], vb_ref[:, lanes])

        @pl.when(is_last)
        def _(vn=vn, lanes=lanes):
            sv_ref[0, :, lanes] = vn[tm - CHUNK:, :]

        ws = jnp.where(row >= col, ws_ref[grp], 0.0).astype(BF16)
        rhs = jnp.concatenate([vn[c * CHUNK:(c + 1) * CHUNK] for c in range(n_chunks)],
                              axis=1).astype(BF16)
        mixed_cat = _dot(ws, rhs)
        mixed = jnp.concatenate(
            [mixed_cat[:, c * gd:(c + 1) * gd] + bs_ref[grp] for c in range(n_chunks)], axis=0)
        gated.append((z[:, lanes] * mixed).astype(BF16))
    a = jnp.concatenate(gated, axis=1)
    o_ref[...] = x + _dot(a, wout_ref[...])


def _gmlp_prompt(x, g, w_in, vn_g, vn_b, w_s, b_s, w_out, *, batch, tm):
    t, d = x.shape
    width = w_out.shape[0]
    gd = width // GMLP_GROUPS
    tiles = t // batch // tm
    bs_b = jnp.broadcast_to(b_s[:, :, None], (GMLP_GROUPS, CHUNK, gd))
    const2 = lambda b, i: (0, 0)
    const3 = lambda b, i: (0, 0, 0)
    return pl.pallas_call(
        _gmlp_prompt_kernel,
        grid=(batch, tiles),
        in_specs=[
            pl.BlockSpec((tm, d), lambda b, i: (b * tiles + i, 0)),
            pl.BlockSpec((1, d), const2),
            pl.BlockSpec((d, 2 * width), const2),
            pl.BlockSpec((1, width), const2),
            pl.BlockSpec((1, width), const2),
            pl.BlockSpec((GMLP_GROUPS, CHUNK, CHUNK), const3),
            pl.BlockSpec((GMLP_GROUPS, CHUNK, gd), const3),
            pl.BlockSpec((width, d), const2),
        ],
        out_specs=[
            pl.BlockSpec((tm, d), lambda b, i: (b * tiles + i, 0)),
            pl.BlockSpec((1, CHUNK, width), lambda b, i: (b, 0, 0)),
        ],
        out_shape=[jax.ShapeDtypeStruct((t, d), F32),
                   jax.ShapeDtypeStruct((batch, CHUNK, width), F32)],
        compiler_params=_params(2),
        name="gmlp_prompt",
    )(x, g.reshape(1, d), w_in, vn_g.reshape(1, width), vn_b.reshape(1, width), w_s, bs_b, w_out)


def _gmlp_sample_kernel(x_ref, g_ref, win_ref, vg_ref, vb_ref, coef_ref, bias_ref, wout_ref,
                        o_ref, sv_ref):
    width = win_ref.shape[1] // 2
    gd = width // GMLP_GROUPS
    x = x_ref[...]
    h = _rms(x, g_ref[...]).astype(BF16)
    z = _gelu_tanh(_dot(h, win_ref[...]))
    vn = jnp.concatenate(
        [_group_layernorm(z[:, width + grp * gd: width + (grp + 1) * gd],
                          vg_ref[:, grp * gd:(grp + 1) * gd], vb_ref[:, grp * gd:(grp + 1) * gd])
         for grp in range(GMLP_GROUPS)], axis=1)
    sv_ref[...] = vn
    mixed = vn * coef_ref[...] + bias_ref[...]
    o_ref[...] = x + _dot((z[:, :width] * mixed).astype(BF16), wout_ref[...])


def _gmlp_sample(x, g, w_in, vn_g, vn_b, w_s, b_s, w_out):
    t, d = x.shape
    width = w_out.shape[0]
    gd = width // GMLP_GROUPS
    coef = jnp.repeat(w_s[:, 0, 0], gd).reshape(1, width)
    bias = jnp.repeat(b_s[:, 0], gd).reshape(1, width)
    return pl.pallas_call(
        _gmlp_sample_kernel,
        out_shape=[jax.ShapeDtypeStruct((t, d), F32), jax.ShapeDtypeStruct((t, width), F32)],
        compiler_params=pltpu.CompilerParams(vmem_limit_bytes=VMEM_LIMIT_BYTES),
        name="gmlp_sample",
    )(x, g.reshape(1, d), w_in, vn_g.reshape(1, width), vn_b.reshape(1, width), coef, bias, w_out)


LOG2_E = 1.4426950408889634
Q_SCALE_LOG2 = HEAD_DIM ** -0.5 * LOG2_E


def _qkv_prompt_kernel(x_ref, g_ref, wt_ref, wk_ref, ktf_ref, vtf_ref, qtb_ref, kab_ref, vtb_ref,
                       mean_ref):
    tm, d = x_ref.shape
    n_pairs = d // LANES
    h = _rms(x_ref[...], g_ref[...]).astype(BF16)
    qkv_t = _dot_nt(wt_ref[...], h)
    ktf_ref[0] = qkv_t[d:2 * d]
    vtf_ref[0] = qkv_t[2 * d:]
    qtb_ref[0, 0] = (qkv_t[:d] * Q_SCALE_LOG2).astype(BF16)
    vtb_ref[0, 0] = qkv_t[2 * d:].astype(BF16)
    k = _dot(h, wk_ref[...])
    mean_ref[0] = jnp.sum(k, axis=0, keepdims=True) * (1.0 / MOBA_BLOCK)
    kb = k.astype(BF16)
    lane = lax.broadcasted_iota(jnp.int32, (tm, LANES), 1)
    onehot = jnp.where(lane == pl.program_id(1), 1.0, 0.0).astype(BF16)
    pieces = []
    for hp in range(n_pairs):
        pieces += [kb[:, hp * LANES:(hp + 1) * LANES], onehot]
    kab_ref[...] = jnp.concatenate(pieces, axis=1)


def _qkv_prompt(x, g, w_t, w_k, *, batch):
    t, d = x.shape
    tm = MOBA_BLOCK
    nb = t // batch // tm
    assert nb <= LANES
    const = lambda b, i: (0, 0)
    feat_spec = pl.BlockSpec((1, d, tm), lambda b, i: (b, 0, i))
    blk_spec = pl.BlockSpec((1, 1, d, tm), lambda b, i: (b, i, 0, 0))
    return pl.pallas_call(
        _qkv_prompt_kernel,
        grid=(batch, nb),
        in_specs=[pl.BlockSpec((tm, d), lambda b, i: (b * nb + i, 0)), pl.BlockSpec((1, d), const),
                  pl.BlockSpec((3 * d, d), const), pl.BlockSpec((d, d), const)],
        out_specs=[feat_spec, feat_spec, blk_spec,
                   pl.BlockSpec((tm, 2 * d), lambda b, i: (b * nb + i, 0)), blk_spec,
                   pl.BlockSpec((1, 1, d), lambda b, i: (b * nb + i, 0, 0))],
        out_shape=[jax.ShapeDtypeStruct((batch, d, nb * tm), F32)] * 2
        + [jax.ShapeDtypeStruct((batch, nb, d, tm), BF16), jax.ShapeDtypeStruct((t, 2 * d), BF16),
           jax.ShapeDtypeStruct((batch, nb, d, tm), BF16), jax.ShapeDtypeStruct((batch * nb, 1, d), F32)],
        compiler_params=_params(2),
        name="qkv_prompt",
    )(x, g.reshape(1, d), w_t, w_k)


def _qkv_sample_kernel(x_ref, g_ref, w_ref, q_ref, k_ref, v_ref):
    d = x_ref.shape[1]
    h = _rms(x_ref[...], g_ref[...]).astype(BF16)
    qkv = _dot(h, w_ref[...])
    q_ref[...] = qkv[:, :d]
    k_ref[...] = qkv[:, d:2 * d]
    v_ref[...] = qkv[:, 2 * d:]


def _qkv_sample(x, g, w):
    t, d = x.shape
    return pl.pallas_call(
        _qkv_sample_kernel,
        out_shape=[jax.ShapeDtypeStruct((t, d), F32)] * 3,
        compiler_params=pltpu.CompilerParams(vmem_limit_bytes=VMEM_LIMIT_BYTES),
        name="qkv_sample",
    )(x, g.reshape(1, d), w)


def _proj_kernel(x_ref, a_ref, w_ref, o_ref):
    o_ref[...] = x_ref[...] + _dot(a_ref[...].astype(BF16), w_ref[...])


def _proj_residual(x, a, w, *, tm):
    t, d = x.shape
    row_spec = pl.BlockSpec((tm, d), lambda i: (i, 0))
    return pl.pallas_call(
        _proj_kernel,
        grid=(t // tm,),
        in_specs=[row_spec, row_spec, pl.BlockSpec((d, d), lambda i: (0, 0))],
        out_specs=row_spec,
        out_shape=jax.ShapeDtypeStruct((t, d), F32),
        compiler_params=_params(1),
        name="proj_residual",
    )(x, a, w)


def _top_k_mask(scores, n_iota, n_valid):
    axis = scores.ndim - 2
    n_total = scores.shape[axis]
    s = jnp.where(n_iota < n_valid, scores, NEG_INF)
    picked = jnp.zeros(scores.shape, F32)
    idxs = []
    for _ in range(MOBA_TOPK):
        best = jnp.max(s, axis=axis, keepdims=True)
        idx = jnp.min(jnp.where(s == best, n_iota, n_total), axis=axis, keepdims=True)
        hit = n_iota == idx
        picked = jnp.where(hit, 1.0, picked)
        s = jnp.where(hit, BELOW_NEG_INF, s)
        idxs.append(idx)
    return picked, idxs


def _select_prompt_kernel(qt_ref, mt_ref, o_ref):
    tq = qt_ref.shape[3]
    nb = mt_ref.shape[1] // N_HEADS
    own = pl.program_id(1)
    st = _dot(mt_ref[0], qt_ref[0, 0])
    st = st.reshape(N_HEADS, nb, tq)
    n_iota = lax.broadcasted_iota(jnp.int32, st.shape, 1)
    picked, _ = _top_k_mask(st, n_iota, own)
    bias = jnp.where(n_iota < own, jnp.where(picked > 0.0, 0.0, NEG_INF), NEG_INF)
    o_ref[0] = bias.astype(BF16)


def _select_prompt(qt, means_t):
    batch, nb, d, tq = qt.shape
    return pl.pallas_call(
        _select_prompt_kernel,
        grid=(batch, nb),
        in_specs=[pl.BlockSpec((1, 1, d, tq), lambda b, i: (b, i, 0, 0)),
                  pl.BlockSpec((1, N_HEADS * nb, d), lambda b, i: (b, 0, 0))],
        out_specs=pl.BlockSpec((1, N_HEADS, nb, tq), lambda b, i: (b, 0, 0, i)),
        out_shape=jax.ShapeDtypeStruct((batch, N_HEADS, nb, nb * tq), BF16),
        compiler_params=_params(2),
        name="select_prompt",
    )(qt, means_t)


PAIRS_PER_STEP = 2
BLOCKS_PER_TRIP = 4
SCORE_LEAD = 8


PAGES_PER_STEP = 16
SUM_ROWS = 16


def _accumulate_block_key_sums(page_refs, sums_ref, part):
    blocks_per_step = PAGES_PER_STEP // PAGES_PER_BLOCK

    @pl.when(part == 0)
    def _():
        sums_ref[...] = jnp.zeros_like(sums_ref)

    lane = lax.broadcasted_iota(jnp.int32, sums_ref.shape[1:], 2)
    acc = sums_ref[0]
    for blk in range(blocks_per_step):
        tot = page_refs[blk * PAGES_PER_BLOCK][0]
        for j in range(1, PAGES_PER_BLOCK):
            tot = tot + page_refs[blk * PAGES_PER_BLOCK + j][0]
        col = jnp.sum(tot, axis=-1, keepdims=True)
        acc = jnp.where(lane == part * blocks_per_step + blk, col, acc)
    sums_ref[0] = acc


def _attn_prompt_kernel(pt_ref, qt_ref, bias_ref, ka_ref, vt_ref, *rest, steps_per_seq):
    del pt_ref
    page_refs = rest[:PAGES_PER_STEP]
    o_ref, sums_ref, qa_ref = rest[PAGES_PER_STEP:]
    _accumulate_block_key_sums(page_refs, sums_ref, _rem(pl.program_id(2), steps_per_seq))

    tq = qt_ref.shape[3]
    nb = bias_ref.shape[2]
    qi = pl.program_id(2)
    n_chains = PAIRS_PER_STEP * HEADS_PER_TILE
    ones_rows = jnp.ones((SUM_ROWS, MOBA_BLOCK), BF16)
    feat = lax.broadcasted_iota(jnp.int32, (LANES, tq), 0)
    zeros_bias_pad = jnp.zeros((LANES - nb, tq), BF16)
    for c in range(n_chains):
        p, hh = divmod(c, HEADS_PER_TILE)
        qt = qt_ref[0, 0, p * LANES:(p + 1) * LANES, :]
        in_head = (feat >= hh * HEAD_DIM) & (feat < (hh + 1) * HEAD_DIM)
        qa_ref[c, :LANES, :] = jnp.where(in_head, qt, jnp.zeros_like(qt))
        qa_ref[c, LANES:, :] = jnp.concatenate([bias_ref[0, c], zeros_bias_pad], axis=0)

    def v_rows(c, blk):
        p, hh = divmod(c, HEADS_PER_TILE)
        r0 = p * LANES + hh * HEAD_DIM
        return jnp.concatenate([vt_ref[0, blk, r0:r0 + HEAD_DIM, :], ones_rows], axis=0)


    start = pl.multiple_of(qi * tq, tq)
    key_pos = lax.broadcasted_iota(jnp.int32, (tq, tq), 0)
    qry_pos = lax.broadcasted_iota(jnp.int32, (tq, tq), 1)
    scores = []
    for c in range(n_chains):
        p = c // HEADS_PER_TILE
        kd = ka_ref[pl.ds(start, tq), p * 2 * LANES:p * 2 * LANES + LANES]
        scores.append(_dot(kd, qa_ref[c, :LANES, :]))
    maxes, probs = [], []
    for c in range(n_chains):
        s = jnp.where(key_pos <= qry_pos, scores[c], NEG_INF).astype(BF16)
        m = jnp.max(s, axis=0, keepdims=True)
        maxes.append(m)
        probs.append(jnp.exp2(s - m))
    state = [x for c in range(n_chains) for x in (maxes[c], _dot(v_rows(c, qi), probs[c]))]

    def body(t, state):
        blocks = [t * BLOCKS_PER_TRIP + j for j in range(BLOCKS_PER_TRIP)]

        def score_product(c, n):
            p = c // HEADS_PER_TILE
            off = pl.multiple_of(n * MOBA_BLOCK, MOBA_BLOCK)
            ka = ka_ref[pl.ds(off, MOBA_BLOCK), p * 2 * LANES:(p + 1) * 2 * LANES]
            return _dot(ka, qa_ref[c]).astype(BF16)

        def online_softmax_update(m, acc, s, v_aug):
            m_new = jnp.maximum(m, jnp.max(s, axis=0, keepdims=True))
            acc = jnp.exp2((m - m_new).astype(F32)) * acc + _dot(v_aug, jnp.exp2(s - m_new))
            return m_new, acc

        units = [(c, n) for n in blocks for c in range(n_chains)]
        chain_state = {c: (state[2 * c], state[2 * c + 1]) for c in range(n_chains)}
        scores = {}
        for k in range(len(units) + SCORE_LEAD):
            if k < len(units):
                scores[k] = score_product(*units[k])
            if k >= SCORE_LEAD:
                c, n = units[k - SCORE_LEAD]
                chain_state[c] = online_softmax_update(*chain_state[c], scores.pop(k - SCORE_LEAD),
                                                       v_rows(c, n))
        return tuple(x for c in range(n_chains) for x in chain_state[c])

    n_trips = (qi + BLOCKS_PER_TRIP - 1) // BLOCKS_PER_TRIP
    state = lax.fori_loop(0, n_trips, body, tuple(state))
    outs = []
    for c in range(n_chains):
        acc = state[2 * c + 1]
        outs.append((acc[:HEAD_DIM] / acc[HEAD_DIM:HEAD_DIM + 1]).T)
    o_ref[...] = jnp.concatenate(outs, axis=1).astype(o_ref.dtype)


def _attn_prompt(qt, bias, ka, vt, cache_kt, page_table, layer_page0):
    batch, nb, d, tq = qt.shape
    s = nb * tq
    n_groups = d // (PAIRS_PER_STEP * LANES)
    gw = PAIRS_PER_STEP * LANES
    n_chains = PAIRS_PER_STEP * HEADS_PER_TILE
    bs, n_pages = page_table.shape
    _, h, dh, page = cache_kt.shape
    steps_per_seq = n_pages // PAGES_PER_STEP
    assert n_pages % PAGES_PER_STEP == 0 and n_pages // PAGES_PER_BLOCK < LANES
    assert nb % steps_per_seq == 0 and batch * n_groups * nb == bs * steps_per_seq, \
        "cache pages must tile the attention grid"
    seqs_per_row = nb // steps_per_seq

    def seq_of(b, g, i):
        return (b * n_groups + g) * seqs_per_row + _div(i, steps_per_seq)

    def page_spec(j):
        def index_map(b, g, i, pt):
            page_no = _rem(i, steps_per_seq) * PAGES_PER_STEP + j
            return (layer_page0 + pt[seq_of(b, g, i), page_no], 0, 0, 0)
        return pl.BlockSpec((1, h, dh, page), index_map)

    return pl.pallas_call(
        functools.partial(_attn_prompt_kernel, steps_per_seq=steps_per_seq),
        grid_spec=pltpu.PrefetchScalarGridSpec(
            num_scalar_prefetch=1,
            grid=(batch, n_groups, nb),
            in_specs=[pl.BlockSpec((1, 1, gw, tq), lambda b, g, i, pt: (b, i, g, 0)),
                      pl.BlockSpec((1, n_chains, nb, tq), lambda b, g, i, pt: (b, g, 0, i)),
                      pl.BlockSpec((s, 2 * gw), lambda b, g, i, pt: (b, g)),
                      pl.BlockSpec((1, nb, gw, tq), lambda b, g, i, pt: (b, 0, g, 0))]
            + [page_spec(j) for j in range(PAGES_PER_STEP)],
            out_specs=[pl.BlockSpec((tq, gw), lambda b, g, i, pt: (b * nb + i, g)),
                       pl.BlockSpec((1, h, dh, LANES),
                                    lambda b, g, i, pt: (seq_of(b, g, i), 0, 0, 0))],
            scratch_shapes=[pltpu.VMEM((n_chains, 2 * LANES, tq), BF16)],
        ),
        out_shape=[jax.ShapeDtypeStruct((batch * s, d), BF16),
                   jax.ShapeDtypeStruct((bs, h, dh, LANES), F32)],
        compiler_params=_params(3),
        name="attn_prompt",
    )(page_table, qt, bias, ka, vt, *([cache_kt] * PAGES_PER_STEP))


def _as_column(row):
    d = row.shape[1]
    return jnp.concatenate(
        [jnp.broadcast_to(row[:, c * LANES:(c + 1) * LANES], (LANES, LANES)).T
         for c in range(d // LANES)], axis=0)


def _select_sample_kernel(q_ref, knew_ref, sums_ref, o_ref, *, nb):
    _, h, dh, _ = sums_ref.shape
    lane3 = lax.broadcasted_iota(jnp.int32, (h, dh, LANES), 2)
    q_col = _as_column(q_ref[0]).reshape(h, dh, LANES)
    sums = jnp.where(lane3 == nb, _as_column(knew_ref[0]).reshape(h, dh, LANES), sums_ref[0])
    scores = jnp.sum(sums * (1.0 / MOBA_BLOCK) * q_col, axis=1)
    n_iota = lax.broadcasted_iota(jnp.int32, scores.shape, 1)
    s = jnp.where(n_iota < nb, scores, NEG_INF)
    picks = jnp.zeros(scores.shape, jnp.int32)
    for r in range(MOBA_TOPK):
        best = jnp.max(s, axis=1, keepdims=True)
        idx = jnp.min(jnp.where(s == best, n_iota, LANES), axis=1, keepdims=True)
        s = jnp.where(n_iota == idx, BELOW_NEG_INF, s)
        picks = jnp.where(n_iota == r, idx, picks)
    o_ref[0] = picks


def _select_sample(q, k_new, sums, nb):
    b, h, dh, _ = sums.shape
    d = h * dh
    return pl.pallas_call(
        functools.partial(_select_sample_kernel, nb=nb),
        grid=(b,),
        in_specs=[pl.BlockSpec((1, 1, d), lambda i: (i, 0, 0)),
                  pl.BlockSpec((1, 1, d), lambda i: (i, 0, 0)),
                  pl.BlockSpec((1, h, dh, LANES), lambda i: (i, 0, 0, 0))],
        out_specs=pl.BlockSpec((1, h, LANES), lambda i: (i, 0, 0)),
        out_shape=jax.ShapeDtypeStruct((b, h, LANES), jnp.int32),
        compiler_params=_params(1),
        name="select_sample",
    )(q, k_new, sums)


SAMPLE_HEADS_PER_STEP = 4
N_SEL_PAGES = SAMPLE_HEADS_PER_STEP * MOBA_TOPK * PAGES_PER_BLOCK


def _attn_sample_kernel(idx_ref, pt_ref, q_ref, knew_ref, vnew_ref, *refs):
    del idx_ref, pt_ref
    k_refs, v_refs, o_ref = refs[:N_SEL_PAGES], refs[N_SEL_PAGES:2 * N_SEL_PAGES], refs[-1]
    rows = 8
    width = SAMPLE_HEADS_PER_STEP * HEAD_DIM
    q = jnp.broadcast_to(q_ref[0] * (HEAD_DIM ** -0.5), (rows, width))
    k_new = jnp.broadcast_to(knew_ref[0], (rows, width))
    v_new = jnp.broadcast_to(vnew_ref[0], (rows, width))
    per_head = MOBA_TOPK * PAGES_PER_BLOCK
    heads = range(SAMPLE_HEADS_PER_STEP)
    cols = [slice(hh * HEAD_DIM, (hh + 1) * HEAD_DIM) for hh in heads]
    logits = [[_dot(q[:, cols[hh]].astype(BF16), k_refs[hh * per_head + c][0, 0].astype(BF16))
               for c in range(per_head)] for hh in heads]
    probs, denoms, accs = [], [], []
    for hh in heads:
        own = jnp.sum(q[:, cols[hh]] * k_new[:, cols[hh]], axis=1, keepdims=True)
        m = own
        for lg in logits[hh]:
            m = jnp.maximum(m, jnp.max(lg, axis=1, keepdims=True))
        p_own = jnp.exp(own - m)
        ps = [jnp.exp(lg - m) for lg in logits[hh]]
        denom = p_own
        for p in ps:
            denom = denom + jnp.sum(p, axis=1, keepdims=True)
        probs.append([p.astype(BF16) for p in ps])
        denoms.append(denom)
        accs.append(p_own * v_new[:, cols[hh]])
    outs = []
    for hh in heads:
        acc = accs[hh]
        for c in range(per_head):
            acc = acc + _dot_nt(probs[hh][c], v_refs[hh * per_head + c][0, 0].astype(BF16))
        outs.append(acc / denoms[hh])
    o_ref[0] = jnp.concatenate(outs, axis=1)[:1]


def _attn_sample(q, k_new, v_new, cache_kt, cache_vt, sel_idx, page_table, layer_page0):
    b, _, d = q.shape
    _, _, dh, page = cache_kt.shape
    n_pages = page_table.shape[0] // b
    n_groups = N_HEADS // SAMPLE_HEADS_PER_STEP
    vec_spec = pl.BlockSpec((1, 1, SAMPLE_HEADS_PER_STEP * HEAD_DIM),
                            lambda bi, hg, idx, pt: (bi, 0, hg))

    def page_spec(hh, r, j):
        def index_map(bi, hg, idx, pt):
            head = hg * SAMPLE_HEADS_PER_STEP + hh
            blk = idx[(bi * N_HEADS + head) * MOBA_TOPK + r]
            return (layer_page0 + pt[bi * n_pages + blk * PAGES_PER_BLOCK + j], head, 0, 0)
        return pl.BlockSpec((1, 1, dh, page), index_map)

    page_specs = [page_spec(hh, r, j) for hh in range(SAMPLE_HEADS_PER_STEP)
                  for r in range(MOBA_TOPK) for j in range(PAGES_PER_BLOCK)]
    return pl.pallas_call(
        _attn_sample_kernel,
        grid_spec=pltpu.PrefetchScalarGridSpec(
            num_scalar_prefetch=2,
            grid=(b, n_groups),
            in_specs=[vec_spec] * 3 + page_specs * 2,
            out_specs=vec_spec,
        ),
        out_shape=jax.ShapeDtypeStruct((b, 1, d), F32),
        compiler_params=_params(2),
        name="attn_sample",
    )(sel_idx, page_table, q, k_new, v_new,
      *([cache_kt] * N_SEL_PAGES), *([cache_vt] * N_SEL_PAGES))


FFN_TILE = 1024
GMLP_TILE = 512
PROJ_TILE = 512


def kernel(x_prompt, x_sample, cache_k, cache_v, page_table, norm_ffn1, ffn1_gate, ffn1_up, ffn1_down, norm_mix, norm_ffn2, ffn2_gate, ffn2_up, ffn2_down, gmlp_in, gmlp_vnorm_g, gmlp_vnorm_b, gmlp_ws, gmlp_bs, gmlp_out, moba_qkv, moba_out, final_norm):
    bp, s, d = x_prompt.shape
    bs, t_new, _ = x_sample.shape
    assert t_new == 1 and s % MOBA_BLOCK == 0 and d == N_HEADS * HEAD_DIM
    depth = norm_ffn1.shape[0]
    n_pool, page = cache_k.shape[1], cache_k.shape[2]
    assert page == PAGE_SIZE
    nb_prompt = s // MOBA_BLOCK

    yp = x_prompt.reshape(bp * s, d)
    ys = x_sample.reshape(bs * t_new, d)
    cache_kt = jnp.transpose(cache_k, (0, 1, 3, 4, 2)).reshape(-1, N_HEADS, HEAD_DIM, PAGE_SIZE)
    cache_vt = jnp.transpose(cache_v, (0, 1, 3, 4, 2)).reshape(-1, N_HEADS, HEAD_DIM, PAGE_SIZE)
    head_of_col = jnp.arange(d) // HEAD_DIM

    sv_p, sv_s, kp_l, vp_l, ks_l, vs_l = [], [], [], [], [], []
    for i in range(depth):
        last = i == depth - 1
        yp = _ffn(yp, norm_ffn1[i], ffn1_gate, ffn1_up, ffn1_down, i, tm=FFN_TILE)
        ys = _ffn(ys, norm_ffn1[i], ffn1_gate, ffn1_up, ffn1_down, i, tm=bs)
        j = i // 2
        if i % 2 == 0:
            w_in, w_out = gmlp_in[j].astype(BF16), gmlp_out[j].astype(BF16)
            yp, v_p = _gmlp_prompt(yp, norm_mix[i], w_in, gmlp_vnorm_g[j], gmlp_vnorm_b[j],
                                   gmlp_ws[j], gmlp_bs[j], w_out, batch=bp, tm=GMLP_TILE)
            ys, v_s = _gmlp_sample(ys, norm_mix[i], w_in, gmlp_vnorm_g[j], gmlp_vnorm_b[j],
                                   gmlp_ws[j], gmlp_bs[j], w_out)
            sv_p.append(v_p.reshape(bp, CHUNK, GMLP_GROUPS, -1))
            sv_s.append(v_s.reshape(bs, t_new, GMLP_GROUPS, -1))
        else:
            w_qkv, w_o = moba_qkv[j].astype(BF16), moba_out[j].astype(BF16)
            ktf, vtf, qt, ka, vt, means = _qkv_prompt(yp, norm_mix[i], w_qkv.T, w_qkv[:, d:2 * d],
                                                      batch=bp)
            means = means.reshape(bp, nb_prompt, d)
            means_t = jnp.where(jnp.arange(N_HEADS)[None, :, None, None] == head_of_col[None, None, None, :],
                                means[:, None, :, :], 0.0).reshape(bp, N_HEADS * nb_prompt, d).astype(BF16)
            bias = _select_prompt(qt, means_t)
            attn_p, sums = _attn_prompt(qt, bias, ka, vt, cache_kt, page_table, j * n_pool)
            yp = _proj_residual(yp, attn_p, w_o, tm=PROJ_TILE)
            kp_l.append(jnp.transpose(ktf.reshape(bp, N_HEADS, HEAD_DIM, s), (0, 3, 1, 2)))
            vp_l.append(jnp.transpose(vtf.reshape(bp, N_HEADS, HEAD_DIM, s), (0, 3, 1, 2)))
            qs, kn, vn = _qkv_sample(ys, norm_mix[i], w_qkv)
            qs3, kn3, vn3 = qs.reshape(bs, 1, d), kn.reshape(bs, 1, d), vn.reshape(bs, 1, d)
            picks = _select_sample(qs3, kn3, sums, page_table.shape[1] // PAGES_PER_BLOCK)
            sel_idx = picks[:, :, :MOBA_TOPK].reshape(-1)
            attn_s = _attn_sample(qs3, kn3, vn3, cache_kt, cache_vt, sel_idx,
                                  page_table.reshape(-1), j * n_pool)
            ys = _proj_residual(ys, attn_s.reshape(bs, d), w_o, tm=bs)
            ks_l.append(kn.reshape(bs, t_new, N_HEADS, HEAD_DIM))
            vs_l.append(vn.reshape(bs, t_new, N_HEADS, HEAD_DIM))
        fg = final_norm if last else None
        yp = _ffn(yp, norm_ffn2[i], ffn2_gate, ffn2_up, ffn2_down, i, fg, tm=FFN_TILE)
        ys = _ffn(ys, norm_ffn2[i], ffn2_gate, ffn2_up, ffn2_down, i, fg, tm=bs)
    return (yp.reshape(bp, s, d), ys.reshape(bs, t_new, d), jnp.stack(sv_p), jnp.stack(sv_s),
            jnp.stack(kp_l), jnp.stack(vp_l), jnp.stack(ks_l), jnp.stack(vs_l))
```

```python
import functools

import jax
import jax.numpy as jnp
from jax import lax
from jax.experimental import pallas as pl
from jax.experimental.pallas import tpu as pltpu

F32 = jnp.float32
BF16 = jnp.bfloat16

EPS = 1e-6
NEG_INF = -1e30
BELOW_NEG_INF = -3e38

CHUNK = 128
GMLP_GROUPS = 8
N_HEADS = 16
HEAD_DIM = 64
MOBA_BLOCK = 256
MOBA_TOPK = 3
PAGE_SIZE = 128
PAGES_PER_BLOCK = MOBA_BLOCK // PAGE_SIZE

LANES = 128
HEADS_PER_TILE = LANES // HEAD_DIM
VMEM_LIMIT_BYTES = 56 * 1024 * 1024


def _params(n_axes):
    return pltpu.CompilerParams(dimension_semantics=("arbitrary",) * n_axes,
                                vmem_limit_bytes=VMEM_LIMIT_BYTES)


def _rms(x, g):
    return x * lax.rsqrt(jnp.mean(x * x, axis=-1, keepdims=True) + EPS) * g


def _gelu_tanh(x):
    return 0.5 * x * (1.0 + jnp.tanh(0.7978845608028654 * (x + 0.044715 * (x * x * x))))


def _div(x, n):
    if n & (n - 1) == 0:
        return lax.shift_right_logical(x, n.bit_length() - 1)
    return lax.div(x, n)


def _rem(x, n):
    if n & (n - 1) == 0:
        return x & (n - 1)
    return lax.rem(x, n)


def _dot(a, b):
    return jnp.dot(a, b, preferred_element_type=F32)


def _dot_nt(a, b):
    return lax.dot_general(a, b, (((1,), (1,)), ((), ())), preferred_element_type=F32)


def _ffn_kernel(x_ref, g_ref, wg_ref, wu_ref, wd_ref, *rest, final_norm, mixer_proj):
    rest = list(rest)
    a_ref, wo_ref = (rest.pop(0), rest.pop(0)) if mixer_proj else (None, None)
    fg_ref = rest.pop(0) if final_norm else None
    o_ref, h_ref, acc_ref = rest[:3]
    res_ref = rest[3] if mixer_proj else x_ref
    j = pl.program_id(1)

    @pl.when(j == 0)
    def _():
        if mixer_proj:
            res_ref[...] = x_ref[...] + _dot(a_ref[...].astype(BF16), wo_ref[...])
        h_ref[...] = _rms(res_ref[...], g_ref[...]).astype(BF16)
        acc_ref[...] = jnp.zeros_like(acc_ref)

    h = h_ref[...]
    gate = _dot(h, wg_ref[...].astype(BF16))
    up = _dot(h, wu_ref[...].astype(BF16))
    act = (gate * jax.nn.sigmoid(gate) * up).astype(BF16)
    acc_ref[...] += _dot(act, wd_ref[...].astype(BF16))

    @pl.when(j == pl.num_programs(1) - 1)
    def _():
        y = res_ref[...] + 0.5 * acc_ref[...]
        if final_norm:
            y = _rms(y, fg_ref[...])
        o_ref[...] = y


def _ffn(x, g, wg, wu, wd, layer, final_g=None, mixer=None, *, tm, tf=256):
    t, d = x.shape
    ff = wg.shape[2]
    row_spec = pl.BlockSpec((tm, d), lambda i, j: (i, 0))
    in_specs = [
        row_spec,
        pl.BlockSpec((1, d), lambda i, j: (0, 0)),
        pl.BlockSpec((None, d, tf), lambda i, j: (layer, 0, j)),
        pl.BlockSpec((None, d, tf), lambda i, j: (layer, 0, j)),
        pl.BlockSpec((None, tf, d), lambda i, j: (layer, j, 0)),
    ]
    args = [x, g.reshape(1, d), wg, wu, wd]
    scratch = [pltpu.VMEM((tm, d), BF16), pltpu.VMEM((tm, d), F32)]
    if mixer is not None:
        in_specs += [row_spec, pl.BlockSpec((d, d), lambda i, j: (0, 0))]
        args += list(mixer)
        scratch.append(pltpu.VMEM((tm, d), F32))
    if final_g is not None:
        in_specs.append(pl.BlockSpec((1, d), lambda i, j: (0, 0)))
        args.append(final_g.reshape(1, d))
    return pl.pallas_call(
        functools.partial(_ffn_kernel, final_norm=final_g is not None, mixer_proj=mixer is not None),
        grid=(t // tm, ff // tf),
        in_specs=in_specs,
        out_specs=row_spec,
        out_shape=jax.ShapeDtypeStruct((t, d), F32),
        scratch_shapes=scratch,
        compiler_params=_params(2),
        name="ffn",
    )(*args)


def _group_layernorm(v, gain, bias):
    mu = jnp.mean(v, axis=-1, keepdims=True)
    d = v - mu
    var = jnp.mean(d * d, axis=-1, keepdims=True)
    return d * lax.rsqrt(var + EPS) * gain + bias


def _gmlp_prompt_kernel(x_ref, g_ref, win_ref, vg_ref, vb_ref, ws_ref, bs_ref, wout_ref,
                        o_ref, sv_ref):
    tm, d = x_ref.shape
    width = win_ref.shape[1] // 2
    gd = width // GMLP_GROUPS
    n_chunks = tm // CHUNK
    x = x_ref[...]
    h = _rms(x, g_ref[...]).astype(BF16)
    z = _gelu_tanh(_dot(h, win_ref[...]))
    row = lax.broadcasted_iota(jnp.int32, (CHUNK, CHUNK), 0)
    col = lax.broadcasted_iota(jnp.int32, (CHUNK, CHUNK), 1)
    is_last = pl.program_id(1) == pl.num_programs(1) - 1
    gated = []
    for grp in range(GMLP_GROUPS):
        lanes = slice(grp * gd, (grp + 1) * gd)
        vn = _group_layernorm(z[:, width + grp * gd: width + (grp + 1) * gd],
                              vg_ref[:, lanes], vb_ref[:, lanes])

        @pl.when(is_last)
        def _(vn=vn, lanes=lanes):
            sv_ref[0, :, lanes] = vn[tm - CHUNK:, :]

        ws = jnp.where(row >= col, ws_ref[grp], 0.0).astype(BF16)
        rhs = jnp.concatenate([vn[c * CHUNK:(c + 1) * CHUNK] for c in range(n_chunks)],
                              axis=1).astype(BF16)
        mixed_cat = _dot(ws, rhs)
        mixed = jnp.concatenate(
            [mixed_cat[:, c * gd:(c + 1) * gd] + bs_ref[grp] for c in range(n_chunks)], axis=0)
        gated.append((z[:, lanes] * mixed).astype(BF16))
    a = jnp.concatenate(gated, axis=1)
    o_ref[...] = x + _dot(a, wout_ref[...])


def _gmlp_prompt(x, g, w_in, vn_g, vn_b, w_s, b_s, w_out, *, batch, tm):
    t, d = x.shape
    width = w_out.shape[0]
    gd = width // GMLP_GROUPS
    tiles = t // batch // tm
    bs_b = jnp.broadcast_to(b_s[:, :, None], (GMLP_GROUPS, CHUNK, gd))
    const2 = lambda b, i: (0, 0)
    const3 = lambda b, i: (0, 0, 0)
    return pl.pallas_call(
        _gmlp_prompt_kernel,
        grid=(batch, tiles),
        in_specs=[
            pl.BlockSpec((tm, d), lambda b, i: (b * tiles + i, 0)),
            pl.BlockSpec((1, d), const2),
            pl.BlockSpec((d, 2 * width), const2),
            pl.BlockSpec((1, width), const2),
            pl.BlockSpec((1, width), const2),
            pl.BlockSpec((GMLP_GROUPS, CHUNK, CHUNK), const3),
            pl.BlockSpec((GMLP_GROUPS, CHUNK, gd), const3),
            pl.BlockSpec((width, d), const2),
        ],
        out_specs=[
            pl.BlockSpec((tm, d), lambda b, i: (b * tiles + i, 0)),
            pl.BlockSpec((1, CHUNK, width), lambda b, i: (b, 0, 0)),
        ],
        out_shape=[jax.ShapeDtypeStruct((t, d), F32),
                   jax.ShapeDtypeStruct((batch, CHUNK, width), F32)],
        compiler_params=_params(2),
        name="gmlp_prompt",
    )(x, g.reshape(1, d), w_in, vn_g.reshape(1, width), vn_b.reshape(1, width), w_s, bs_b, w_out)


def _gmlp_sample_kernel(x_ref, g_ref, win_ref, vg_ref, vb_ref, coef_ref, bias_ref, wout_ref,
                        o_ref, sv_ref):
    width = win_ref.shape[1] // 2
    gd = width // GMLP_GROUPS
    x = x_ref[...]
    h = _rms(x, g_ref[...]).astype(BF16)
    z = _gelu_tanh(_dot(h, win_ref[...]))
    vn = jnp.concatenate(
        [_group_layernorm(z[:, width + grp * gd: width + (grp + 1) * gd],
                          vg_ref[:, grp * gd:(grp + 1) * gd], vb_ref[:, grp * gd:(grp + 1) * gd])
         for grp in range(GMLP_GROUPS)], axis=1)
    sv_ref[...] = vn
    mixed = vn * coef_ref[...] + bias_ref[...]
    o_ref[...] = x + _dot((z[:, :width] * mixed).astype(BF16), wout_ref[...])


def _gmlp_sample(x, g, w_in, vn_g, vn_b, w_s, b_s, w_out):
    t, d = x.shape
    width = w_out.shape[0]
    gd = width // GMLP_GROUPS
    coef = jnp.repeat(w_s[:, 0, 0], gd).reshape(1, width)
    bias = jnp.repeat(b_s[:, 0], gd).reshape(1, width)
    return pl.pallas_call(
        _gmlp_sample_kernel,
        out_shape=[jax.ShapeDtypeStruct((t, d), F32), jax.ShapeDtypeStruct((t, width), F32)],
        compiler_params=pltpu.CompilerParams(vmem_limit_bytes=VMEM_LIMIT_BYTES),
        name="gmlp_sample",
    )(x, g.reshape(1, d), w_in, vn_g.reshape(1, width), vn_b.reshape(1, width), coef, bias, w_out)


LOG2_E = 1.4426950408889634
Q_SCALE_LOG2 = HEAD_DIM ** -0.5 * LOG2_E


def _qkv_prompt_kernel(x_ref, g_ref, wt_ref, wk_ref, ktf_ref, vtf_ref, qtb_ref, kab_ref, vtb_ref,
                       mean_ref):
    tm, d = x_ref.shape
    n_pairs = d // LANES
    h = _rms(x_ref[...], g_ref[...]).astype(BF16)
    qkv_t = _dot_nt(wt_ref[...], h)
    ktf_ref[0] = qkv_t[d:2 * d]
    vtf_ref[0] = qkv_t[2 * d:]
    qtb_ref[0, 0] = (qkv_t[:d] * Q_SCALE_LOG2).astype(BF16)
    vtb_ref[0, 0] = qkv_t[2 * d:].astype(BF16)
    k = _dot(h, wk_ref[...])
    mean_ref[0] = jnp.sum(k, axis=0, keepdims=True) * (1.0 / MOBA_BLOCK)
    kb = k.astype(BF16)
    lane = lax.broadcasted_iota(jnp.int32, (tm, LANES), 1)
    onehot = jnp.where(lane == pl.program_id(1), 1.0, 0.0).astype(BF16)
    pieces = []
    for hp in range(n_pairs):
        pieces += [kb[:, hp * LANES:(hp + 1) * LANES], onehot]
    kab_ref[...] = jnp.concatenate(pieces, axis=1)


def _qkv_prompt(x, g, w_t, w_k, *, batch):
    t, d = x.shape
    tm = MOBA_BLOCK
    nb = t // batch // tm
    assert nb <= LANES
    const = lambda b, i: (0, 0)
    feat_spec = pl.BlockSpec((1, d, tm), lambda b, i: (b, 0, i))
    blk_spec = pl.BlockSpec((1, 1, d, tm), lambda b, i: (b, i, 0, 0))
    return pl.pallas_call(
        _qkv_prompt_kernel,
        grid=(batch, nb),
        in_specs=[pl.BlockSpec((tm, d), lambda b, i: (b * nb + i, 0)), pl.BlockSpec((1, d), const),
                  pl.BlockSpec((3 * d, d), const), pl.BlockSpec((d, d), const)],
        out_specs=[feat_spec, feat_spec, blk_spec,
                   pl.BlockSpec((tm, 2 * d), lambda b, i: (b * nb + i, 0)), blk_spec,
                   pl.BlockSpec((1, 1, d), lambda b, i: (b * nb + i, 0, 0))],
        out_shape=[jax.ShapeDtypeStruct((batch, d, nb * tm), F32)] * 2
        + [jax.ShapeDtypeStruct((batch, nb, d, tm), BF16), jax.ShapeDtypeStruct((t, 2 * d), BF16),
           jax.ShapeDtypeStruct((batch, nb, d, tm), BF16), jax.ShapeDtypeStruct((batch * nb, 1, d), F32)],
        compiler_params=_params(2),
        name="qkv_prompt",
    )(x, g.reshape(1, d), w_t, w_k)


def _qkv_sample_kernel(x_ref, g_ref, w_ref, q_ref, k_ref, v_ref):
    d = x_ref.shape[1]
    h = _rms(x_ref[...], g_ref[...]).astype(BF16)
    qkv = _dot(h, w_ref[...])
    q_ref[...] = qkv[:, :d]
    k_ref[...] = qkv[:, d:2 * d]
    v_ref[...] = qkv[:, 2 * d:]


def _qkv_sample(x, g, w):
    t, d = x.shape
    return pl.pallas_call(
        _qkv_sample_kernel,
        out_shape=[jax.ShapeDtypeStruct((t, d), F32)] * 3,
        compiler_params=pltpu.CompilerParams(vmem_limit_bytes=VMEM_LIMIT_BYTES),
        name="qkv_sample",
    )(x, g.reshape(1, d), w)


def _top_k_mask(scores, n_iota, n_valid):
    axis = scores.ndim - 2
    n_total = scores.shape[axis]
    s = jnp.where(n_iota < n_valid, scores, NEG_INF)
    picked = jnp.zeros(scores.shape, F32)
    idxs = []
    for _ in range(MOBA_TOPK):
        best = jnp.max(s, axis=axis, keepdims=True)
        idx = jnp.min(jnp.where(s == best, n_iota, n_total), axis=axis, keepdims=True)
        hit = n_iota == idx
        picked = jnp.where(hit, 1.0, picked)
        s = jnp.where(hit, BELOW_NEG_INF, s)
        idxs.append(idx)
    return picked, idxs


def _select_prompt_kernel(qt_ref, mt_ref, o_ref):
    tq = qt_ref.shape[3]
    nb = mt_ref.shape[1] // N_HEADS
    own = pl.program_id(1)
    st = _dot(mt_ref[0], qt_ref[0, 0])
    st = st.reshape(N_HEADS, nb, tq)
    n_iota = lax.broadcasted_iota(jnp.int32, st.shape, 1)
    picked, _ = _top_k_mask(st, n_iota, own)
    bias = jnp.where(n_iota < own, jnp.where(picked > 0.0, 0.0, NEG_INF), NEG_INF)
    o_ref[0] = bias.astype(BF16)


def _select_prompt(qt, means_t):
    batch, nb, d, tq = qt.shape
    return pl.pallas_call(
        _select_prompt_kernel,
        grid=(batch, nb),
        in_specs=[pl.BlockSpec((1, 1, d, tq), lambda b, i: (b, i, 0, 0)),
                  pl.BlockSpec((1, N_HEADS * nb, d), lambda b, i: (b, 0, 0))],
        out_specs=pl.BlockSpec((1, N_HEADS, nb, tq), lambda b, i: (b, 0, 0, i)),
        out_shape=jax.ShapeDtypeStruct((batch, N_HEADS, nb, nb * tq), BF16),
        compiler_params=_params(2),
        name="select_prompt",
    )(qt, means_t)


PAIRS_PER_STEP = 2
BLOCKS_PER_TRIP = 4
SCORE_LEAD = 8


PAGES_PER_STEP = 16
SUM_ROWS = 16


def _accumulate_block_key_sums(page_refs, sums_ref, part):
    blocks_per_step = PAGES_PER_STEP // PAGES_PER_BLOCK

    @pl.when(part == 0)
    def _():
        sums_ref[...] = jnp.zeros_like(sums_ref)

    lane = lax.broadcasted_iota(jnp.int32, sums_ref.shape[1:], 2)
    acc = sums_ref[0]
    for blk in range(blocks_per_step):
        tot = page_refs[blk * PAGES_PER_BLOCK][0]
        for j in range(1, PAGES_PER_BLOCK):
            tot = tot + page_refs[blk * PAGES_PER_BLOCK + j][0]
        col = jnp.sum(tot, axis=-1, keepdims=True)
        acc = jnp.where(lane == part * blocks_per_step + blk, col, acc)
    sums_ref[0] = acc


def _attn_prompt_kernel(pt_ref, qt_ref, bias_ref, ka_ref, vt_ref, *rest, steps_per_seq):
    del pt_ref
    page_refs = rest[:PAGES_PER_STEP]
    o_ref, sums_ref, qa_ref = rest[PAGES_PER_STEP:]
    _accumulate_block_key_sums(page_refs, sums_ref, _rem(pl.program_id(2), steps_per_seq))

    tq = qt_ref.shape[3]
    nb = bias_ref.shape[2]
    qi = pl.program_id(2)
    n_chains = PAIRS_PER_STEP * HEADS_PER_TILE
    ones_rows = jnp.ones((SUM_ROWS, MOBA_BLOCK), BF16)
    feat = lax.broadcasted_iota(jnp.int32, (LANES, tq), 0)
    zeros_bias_pad = jnp.zeros((LANES - nb, tq), BF16)
    for c in range(n_chains):
        p, hh = divmod(c, HEADS_PER_TILE)
        qt = qt_ref[0, 0, p * LANES:(p + 1) * LANES, :]
        in_head = (feat >= hh * HEAD_DIM) & (feat < (hh + 1) * HEAD_DIM)
        qa_ref[c, :LANES, :] = jnp.where(in_head, qt, jnp.zeros_like(qt))
        qa_ref[c, LANES:, :] = jnp.concatenate([bias_ref[0, c], zeros_bias_pad], axis=0)

    def v_rows(c, blk):
        p, hh = divmod(c, HEADS_PER_TILE)
        r0 = p * LANES + hh * HEAD_DIM
        return jnp.concatenate([vt_ref[0, blk, r0:r0 + HEAD_DIM, :], ones_rows], axis=0)


    start = pl.multiple_of(qi * tq, tq)
    key_pos = lax.broadcasted_iota(jnp.int32, (tq, tq), 0)
    qry_pos = lax.broadcasted_iota(jnp.int32, (tq, tq), 1)
    scores = []
    for c in range(n_chains):
        p = c // HEADS_PER_TILE
        kd = ka_ref[pl.ds(start, tq), p * 2 * LANES:p * 2 * LANES + LANES]
        scores.append(_dot(kd, qa_ref[c, :LANES, :]))
    maxes, probs = [], []
    for c in range(n_chains):
        s = jnp.where(key_pos <= qry_pos, scores[c], NEG_INF).astype(BF16)
        m = jnp.max(s, axis=0, keepdims=True)
        maxes.append(m)
        probs.append(jnp.exp2(s - m))
    state = [x for c in range(n_chains) for x in (maxes[c], _dot(v_rows(c, qi), probs[c]))]

    def trip(first_block, n_blocks, state):
        blocks = [first_block + j for j in range(n_blocks)]
        lead = min(SCORE_LEAD, n_blocks * n_chains // 2)

        def score_product(c, n):
            p = c // HEADS_PER_TILE
            off = pl.multiple_of(n * MOBA_BLOCK, MOBA_BLOCK)
            ka = ka_ref[pl.ds(off, MOBA_BLOCK), p * 2 * LANES:(p + 1) * 2 * LANES]
            return _dot(ka, qa_ref[c]).astype(BF16)

        def online_softmax_update(m, acc, s, v_aug):
            m_new = jnp.maximum(m, jnp.max(s, axis=0, keepdims=True))
            acc = jnp.exp2((m - m_new).astype(F32)) * acc + _dot(v_aug, jnp.exp2(s - m_new))
            return m_new, acc

        units = [(c, n) for n in blocks for c in range(n_chains)]
        chain_state = {c: (state[2 * c], state[2 * c + 1]) for c in range(n_chains)}
        scores = {}
        for k in range(len(units) + lead):
            if k < len(units):
                scores[k] = score_product(*units[k])
            if k >= lead:
                c, n = units[k - lead]
                chain_state[c] = online_softmax_update(*chain_state[c], scores.pop(k - lead),
                                                       v_rows(c, n))
        return tuple(x for c in range(n_chains) for x in chain_state[c])

    n_trips = _div(qi + BLOCKS_PER_TRIP - 1, BLOCKS_PER_TRIP)
    state = lax.fori_loop(0, n_trips, lambda t, st: trip(t * BLOCKS_PER_TRIP, BLOCKS_PER_TRIP, st),
                          tuple(state))
    outs = []
    for c in range(n_chains):
        acc = state[2 * c + 1]
        outs.append((acc[:HEAD_DIM] / acc[HEAD_DIM:HEAD_DIM + 1]).T)
    o_ref[...] = jnp.concatenate(outs, axis=1).astype(o_ref.dtype)


def _attn_prompt(qt, bias, ka, vt, cache_kt, page_table, layer_page0):
    batch, nb, d, tq = qt.shape
    s = nb * tq
    n_groups = d // (PAIRS_PER_STEP * LANES)
    gw = PAIRS_PER_STEP * LANES
    n_chains = PAIRS_PER_STEP * HEADS_PER_TILE
    bs, n_pages = page_table.shape
    _, h, dh, page = cache_kt.shape
    steps_per_seq = n_pages // PAGES_PER_STEP
    assert nb % BLOCKS_PER_TRIP == 0 and nb <= LANES
    assert n_pages % PAGES_PER_STEP == 0 and n_pages // PAGES_PER_BLOCK < LANES
    assert nb % steps_per_seq == 0 and batch * n_groups * nb == bs * steps_per_seq, \
        "cache pages must tile the attention grid"
    seqs_per_row = nb // steps_per_seq

    def seq_of(b, g, i):
        return (b * n_groups + g) * seqs_per_row + _div(i, steps_per_seq)

    def page_spec(j):
        def index_map(b, g, i, pt):
            page_no = _rem(i, steps_per_seq) * PAGES_PER_STEP + j
            return (layer_page0 + pt[seq_of(b, g, i), page_no], 0, 0, 0)
        return pl.BlockSpec((1, h, dh, page), index_map)

    return pl.pallas_call(
        functools.partial(_attn_prompt_kernel, steps_per_seq=steps_per_seq),
        grid_spec=pltpu.PrefetchScalarGridSpec(
            num_scalar_prefetch=1,
            grid=(batch, n_groups, nb),
            in_specs=[pl.BlockSpec((1, 1, gw, tq), lambda b, g, i, pt: (b, i, g, 0)),
                      pl.BlockSpec((1, n_chains, nb, tq), lambda b, g, i, pt: (b, g, 0, i)),
                      pl.BlockSpec((s, 2 * gw), lambda b, g, i, pt: (b, g)),
                      pl.BlockSpec((1, nb, gw, tq), lambda b, g, i, pt: (b, 0, g, 0))]
            + [page_spec(j) for j in range(PAGES_PER_STEP)],
            out_specs=[pl.BlockSpec((tq, gw), lambda b, g, i, pt: (b * nb + i, g)),
                       pl.BlockSpec((1, h, dh, LANES),
                                    lambda b, g, i, pt: (seq_of(b, g, i), 0, 0, 0))],
            scratch_shapes=[pltpu.VMEM((n_chains, 2 * LANES, tq), BF16)],
        ),
        out_shape=[jax.ShapeDtypeStruct((batch * s, d), BF16),
                   jax.ShapeDtypeStruct((bs, h, dh, LANES), F32)],
        compiler_params=_params(3),
        name="attn_prompt",
    )(page_table, qt, bias, ka, vt, *([cache_kt] * PAGES_PER_STEP))


def _as_column(row):
    d = row.shape[1]
    return jnp.concatenate(
        [jnp.broadcast_to(row[:, c * LANES:(c + 1) * LANES], (LANES, LANES)).T
         for c in range(d // LANES)], axis=0)


def _select_sample_kernel(q_ref, knew_ref, sums_ref, o_ref, *, nb):
    _, h, dh, _ = sums_ref.shape
    lane3 = lax.broadcasted_iota(jnp.int32, (h, dh, LANES), 2)
    q_col = _as_column(q_ref[0]).reshape(h, dh, LANES)
    sums = jnp.where(lane3 == nb, _as_column(knew_ref[0]).reshape(h, dh, LANES), sums_ref[0])
    scores = jnp.sum(sums * (1.0 / MOBA_BLOCK) * q_col, axis=1)
    n_iota = lax.broadcasted_iota(jnp.int32, scores.shape, 1)
    s = jnp.where(n_iota < nb, scores, NEG_INF)
    picks = jnp.zeros(scores.shape, jnp.int32)
    for r in range(MOBA_TOPK):
        best = jnp.max(s, axis=1, keepdims=True)
        idx = jnp.min(jnp.where(s == best, n_iota, LANES), axis=1, keepdims=True)
        s = jnp.where(n_iota == idx, BELOW_NEG_INF, s)
        picks = jnp.where(n_iota == r, idx, picks)
    o_ref[0] = picks


def _select_sample(q, k_new, sums, nb):
    b, h, dh, _ = sums.shape
    d = h * dh
    return pl.pallas_call(
        functools.partial(_select_sample_kernel, nb=nb),
        grid=(b,),
        in_specs=[pl.BlockSpec((1, 1, d), lambda i: (i, 0, 0)),
                  pl.BlockSpec((1, 1, d), lambda i: (i, 0, 0)),
                  pl.BlockSpec((1, h, dh, LANES), lambda i: (i, 0, 0, 0))],
        out_specs=pl.BlockSpec((1, h, LANES), lambda i: (i, 0, 0)),
        out_shape=jax.ShapeDtypeStruct((b, h, LANES), jnp.int32),
        compiler_params=_params(1),
        name="select_sample",
    )(q, k_new, sums)


SAMPLE_HEADS_PER_STEP = 4
N_SEL_PAGES = SAMPLE_HEADS_PER_STEP * MOBA_TOPK * PAGES_PER_BLOCK


def _attn_sample_kernel(idx_ref, pt_ref, q_ref, knew_ref, vnew_ref, *refs):
    del idx_ref, pt_ref
    k_refs, v_refs, o_ref = refs[:N_SEL_PAGES], refs[N_SEL_PAGES:2 * N_SEL_PAGES], refs[-1]
    rows = 8
    width = SAMPLE_HEADS_PER_STEP * HEAD_DIM
    q = jnp.broadcast_to(q_ref[0] * (HEAD_DIM ** -0.5), (rows, width))
    k_new = jnp.broadcast_to(knew_ref[0], (rows, width))
    v_new = jnp.broadcast_to(vnew_ref[0], (rows, width))
    per_head = MOBA_TOPK * PAGES_PER_BLOCK
    heads = range(SAMPLE_HEADS_PER_STEP)
    cols = [slice(hh * HEAD_DIM, (hh + 1) * HEAD_DIM) for hh in heads]
    logits = [[_dot(q[:, cols[hh]].astype(BF16), k_refs[hh * per_head + c][0, 0].astype(BF16))
               for c in range(per_head)] for hh in heads]
    probs, denoms, accs = [], [], []
    for hh in heads:
        own = jnp.sum(q[:, cols[hh]] * k_new[:, cols[hh]], axis=1, keepdims=True)
        m = own
        for lg in logits[hh]:
            m = jnp.maximum(m, jnp.max(lg, axis=1, keepdims=True))
        p_own = jnp.exp(own - m)
        ps = [jnp.exp(lg - m) for lg in logits[hh]]
        denom = p_own
        for p in ps:
            denom = denom + jnp.sum(p, axis=1, keepdims=True)
        probs.append([p.astype(BF16) for p in ps])
        denoms.append(denom)
        accs.append(p_own * v_new[:, cols[hh]])
    outs = []
    for hh in heads:
        acc = accs[hh]
        for c in range(per_head):
            acc = acc + _dot_nt(probs[hh][c], v_refs[hh * per_head + c][0, 0].astype(BF16))
        outs.append(acc / denoms[hh])
    o_ref[0] = jnp.concatenate(outs, axis=1)[:1]


def _attn_sample(q, k_new, v_new, cache_kt, cache_vt, sel_idx, page_table, layer_page0):
    b, _, d = q.shape
    _, _, dh, page = cache_kt.shape
    n_pages = page_table.shape[0] // b
    n_groups = N_HEADS // SAMPLE_HEADS_PER_STEP
    vec_spec = pl.BlockSpec((1, 1, SAMPLE_HEADS_PER_STEP * HEAD_DIM),
                            lambda bi, hg, idx, pt: (bi, 0, hg))

    def page_spec(hh, r, j):
        def index_map(bi, hg, idx, pt):
            head = hg * SAMPLE_HEADS_PER_STEP + hh
            blk = idx[(bi * N_HEADS + head) * MOBA_TOPK + r]
            return (layer_page0 + pt[bi * n_pages + blk * PAGES_PER_BLOCK + j], head, 0, 0)
        return pl.BlockSpec((1, 1, dh, page), index_map)

    page_specs = [page_spec(hh, r, j) for hh in range(SAMPLE_HEADS_PER_STEP)
                  for r in range(MOBA_TOPK) for j in range(PAGES_PER_BLOCK)]
    return pl.pallas_call(
        _attn_sample_kernel,
        grid_spec=pltpu.PrefetchScalarGridSpec(
            num_scalar_prefetch=2,
            grid=(b, n_groups),
            in_specs=[vec_spec] * 3 + page_specs * 2,
            out_specs=vec_spec,
        ),
        out_shape=jax.ShapeDtypeStruct((b, 1, d), F32),
        compiler_params=_params(2),
        name="attn_sample",
    )(sel_idx, page_table, q, k_new, v_new,
      *([cache_kt] * N_SEL_PAGES), *([cache_vt] * N_SEL_PAGES))


FFN_TILE = 1024
GMLP_TILE = 512


def kernel(x_prompt, x_sample, cache_k, cache_v, page_table, norm_ffn1, ffn1_gate, ffn1_up, ffn1_down, norm_mix, norm_ffn2, ffn2_gate, ffn2_up, ffn2_down, gmlp_in, gmlp_vnorm_g, gmlp_vnorm_b, gmlp_ws, gmlp_bs, gmlp_out, moba_qkv, moba_out, final_norm):
    bp, s, d = x_prompt.shape
    bs, t_new, _ = x_sample.shape
    assert t_new == 1 and s % MOBA_BLOCK == 0 and d == N_HEADS * HEAD_DIM
    depth = norm_ffn1.shape[0]
    n_pool, page = cache_k.shape[1], cache_k.shape[2]
    assert page == PAGE_SIZE
    nb_prompt = s // MOBA_BLOCK

    yp = x_prompt.reshape(bp * s, d)
    ys = x_sample.reshape(bs * t_new, d)
    cache_kt = jnp.transpose(cache_k, (0, 1, 3, 4, 2)).reshape(-1, N_HEADS, HEAD_DIM, PAGE_SIZE)
    cache_vt = jnp.transpose(cache_v, (0, 1, 3, 4, 2)).reshape(-1, N_HEADS, HEAD_DIM, PAGE_SIZE)
    head_of_col = jnp.arange(d) // HEAD_DIM

    sv_p, sv_s, kp_l, vp_l, ks_l, vs_l = [], [], [], [], [], []
    for i in range(depth):
        last = i == depth - 1
        yp = _ffn(yp, norm_ffn1[i], ffn1_gate, ffn1_up, ffn1_down, i, tm=FFN_TILE)
        ys = _ffn(ys, norm_ffn1[i], ffn1_gate, ffn1_up, ffn1_down, i, tm=bs)
        j = i // 2
        mix_p = mix_s = None
        if i % 2 == 0:
            w_in, w_out = gmlp_in[j].astype(BF16), gmlp_out[j].astype(BF16)
            yp, v_p = _gmlp_prompt(yp, norm_mix[i], w_in, gmlp_vnorm_g[j], gmlp_vnorm_b[j],
                                   gmlp_ws[j], gmlp_bs[j], w_out, batch=bp, tm=GMLP_TILE)
            ys, v_s = _gmlp_sample(ys, norm_mix[i], w_in, gmlp_vnorm_g[j], gmlp_vnorm_b[j],
                                   gmlp_ws[j], gmlp_bs[j], w_out)
            sv_p.append(v_p.reshape(bp, CHUNK, GMLP_GROUPS, -1))
            sv_s.append(v_s.reshape(bs, t_new, GMLP_GROUPS, -1))
        else:
            w_qkv, w_o = moba_qkv[j].astype(BF16), moba_out[j].astype(BF16)
            ktf, vtf, qt, ka, vt, means = _qkv_prompt(yp, norm_mix[i], w_qkv.T, w_qkv[:, d:2 * d],
                                                      batch=bp)
            means = means.reshape(bp, nb_prompt, d)
            means_t = jnp.where(jnp.arange(N_HEADS)[None, :, None, None] == head_of_col[None, None, None, :],
                                means[:, None, :, :], 0.0).reshape(bp, N_HEADS * nb_prompt, d).astype(BF16)
            bias = _select_prompt(qt, means_t)
            attn_p, sums = _attn_prompt(qt, bias, ka, vt, cache_kt, page_table, j * n_pool)
            mix_p = (attn_p, w_o)
            kp_l.append(jnp.transpose(ktf.reshape(bp, N_HEADS, HEAD_DIM, s), (0, 3, 1, 2)))
            vp_l.append(jnp.transpose(vtf.reshape(bp, N_HEADS, HEAD_DIM, s), (0, 3, 1, 2)))
            qs, kn, vn = _qkv_sample(ys, norm_mix[i], w_qkv)
            qs3, kn3, vn3 = qs.reshape(bs, 1, d), kn.reshape(bs, 1, d), vn.reshape(bs, 1, d)
            picks = _select_sample(qs3, kn3, sums, page_table.shape[1] // PAGES_PER_BLOCK)
            sel_idx = picks[:, :, :MOBA_TOPK].reshape(-1)
            attn_s = _attn_sample(qs3, kn3, vn3, cache_kt, cache_vt, sel_idx,
                                  page_table.reshape(-1), j * n_pool)
            mix_s = (attn_s.reshape(bs, d), w_o)
            ks_l.append(kn.reshape(bs, t_new, N_HEADS, HEAD_DIM))
            vs_l.append(vn.reshape(bs, t_new, N_HEADS, HEAD_DIM))
        fg = final_norm if last else None
        yp = _ffn(yp, norm_ffn2[i], ffn2_gate, ffn2_up, ffn2_down, i, fg, mix_p, tm=FFN_TILE)
        ys = _ffn(ys, norm_ffn2[i], ffn2_gate, ffn2_up, ffn2_down, i, fg, mix_s, tm=bs)
    return (yp.reshape(bp, s, d), ys.reshape(bs, t_new, d), jnp.stack(sv_p), jnp.stack(sv_s),
            jnp.stack(kp_l), jnp.stack(vp_l), jnp.stack(ks_l), jnp.stack(vs_l))
```

```python
import functools

import jax
import jax.numpy as jnp
from jax import lax
from jax.experimental import pallas as pl
from jax.experimental.pallas import tpu as pltpu

F32 = jnp.float32
BF16 = jnp.bfloat16

EPS = 1e-6
NEG_INF = -1e30
BELOW_NEG_INF = -3e38

CHUNK = 128
GMLP_GROUPS = 8
N_HEADS = 16
HEAD_DIM = 64
MOBA_BLOCK = 256
MOBA_TOPK = 3
PAGE_SIZE = 128
PAGES_PER_BLOCK = MOBA_BLOCK // PAGE_SIZE

LANES = 128
HEADS_PER_TILE = LANES // HEAD_DIM
VMEM_LIMIT_BYTES = 56 * 1024 * 1024


def _params(n_axes):
    return pltpu.CompilerParams(dimension_semantics=("arbitrary",) * n_axes,
                                vmem_limit_bytes=VMEM_LIMIT_BYTES)


def _rms(x, g):
    return x * lax.rsqrt(jnp.mean(x * x, axis=-1, keepdims=True) + EPS) * g


def _gelu_tanh(x):
    return 0.5 * x * (1.0 + jnp.tanh(0.7978845608028654 * (x + 0.044715 * (x * x * x))))


def _div(x, n):
    if n & (n - 1) == 0:
        return lax.shift_right_logical(x, n.bit_length() - 1)
    return lax.div(x, n)


def _rem(x, n):
    if n & (n - 1) == 0:
        return x & (n - 1)
    return lax.rem(x, n)


def _dot(a, b):
    return jnp.dot(a, b, preferred_element_type=F32)


def _dot_nt(a, b):
    return lax.dot_general(a, b, (((1,), (1,)), ((), ())), preferred_element_type=F32)


def _ffn_kernel(x_ref, g_ref, wg_ref, wu_ref, wd_ref, *rest, final_norm, mixer_proj):
    rest = list(rest)
    a_ref, wo_ref = (rest.pop(0), rest.pop(0)) if mixer_proj else (None, None)
    fg_ref = rest.pop(0) if final_norm else None
    o_ref, h_ref, acc_ref = rest[:3]
    res_ref = rest[3] if mixer_proj else x_ref
    j = pl.program_id(1)

    @pl.when(j == 0)
    def _():
        if mixer_proj:
            res_ref[...] = x_ref[...] + _dot(a_ref[...].astype(BF16), wo_ref[...])
        h_ref[...] = _rms(res_ref[...], g_ref[...]).astype(BF16)
        acc_ref[...] = jnp.zeros_like(acc_ref)

    h = h_ref[...]
    gate = _dot(h, wg_ref[...].astype(BF16))
    up = _dot(h, wu_ref[...].astype(BF16))
    act = (gate * jax.nn.sigmoid(gate) * up).astype(BF16)
    acc_ref[...] += _dot(act, wd_ref[...].astype(BF16))

    @pl.when(j == pl.num_programs(1) - 1)
    def _():
        y = res_ref[...] + 0.5 * acc_ref[...]
        if final_norm:
            y = _rms(y, fg_ref[...])
        o_ref[...] = y


def _ffn(x, g, wg, wu, wd, layer, final_g=None, mixer=None, *, tm, tf=256):
    t, d = x.shape
    ff = wg.shape[2]
    row_spec = pl.BlockSpec((tm, d), lambda i, j: (i, 0))
    in_specs = [
        row_spec,
        pl.BlockSpec((1, d), lambda i, j: (0, 0)),
        pl.BlockSpec((None, d, tf), lambda i, j: (layer, 0, j)),
        pl.BlockSpec((None, d, tf), lambda i, j: (layer, 0, j)),
        pl.BlockSpec((None, tf, d), lambda i, j: (layer, j, 0)),
    ]
    args = [x, g.reshape(1, d), wg, wu, wd]
    scratch = [pltpu.VMEM((tm, d), BF16), pltpu.VMEM((tm, d), F32)]
    if mixer is not None:
        in_specs += [row_spec, pl.BlockSpec((d, d), lambda i, j: (0, 0))]
        args += list(mixer)
        scratch.append(pltpu.VMEM((tm, d), F32))
    if final_g is not None:
        in_specs.append(pl.BlockSpec((1, d), lambda i, j: (0, 0)))
        args.append(final_g.reshape(1, d))
    return pl.pallas_call(
        functools.partial(_ffn_kernel, final_norm=final_g is not None, mixer_proj=mixer is not None),
        grid=(t // tm, ff // tf),
        in_specs=in_specs,
        out_specs=row_spec,
        out_shape=jax.ShapeDtypeStruct((t, d), F32),
        scratch_shapes=scratch,
        compiler_params=_params(2),
        name="ffn",
    )(*args)


def _group_layernorm(v, gain, bias):
    mu = jnp.mean(v, axis=-1, keepdims=True)
    d = v - mu
    var = jnp.mean(d * d, axis=-1, keepdims=True)
    return d * lax.rsqrt(var + EPS) * gain + bias


def _gmlp_prompt_kernel(x_ref, g_ref, win_ref, vg_ref, vb_ref, ws_ref, bs_ref, wout_ref,
                        o_ref, sv_ref):
    tm, d = x_ref.shape
    width = win_ref.shape[1] // 2
    gd = width // GMLP_GROUPS
    n_chunks = tm // CHUNK
    x = x_ref[...]
    h = _rms(x, g_ref[...]).astype(BF16)
    z = _gelu_tanh(_dot(h, win_ref[...]))
    row = lax.broadcasted_iota(jnp.int32, (CHUNK, CHUNK), 0)
    col = lax.broadcasted_iota(jnp.int32, (CHUNK, CHUNK), 1)
    is_last = pl.program_id(1) == pl.num_programs(1) - 1
    gated = []
    for grp in range(GMLP_GROUPS):
        lanes = slice(grp * gd, (grp + 1) * gd)
        vn = _group_layernorm(z[:, width + grp * gd: width + (grp + 1) * gd],
                              vg_ref[:, lanes], vb_ref[:, lanes])

        @pl.when(is_last)
        def _(vn=vn, lanes=lanes):
            sv_ref[0, :, lanes] = vn[tm - CHUNK:, :]

        ws = jnp.where(row >= col, ws_ref[grp], 0.0).astype(BF16)
        rhs = jnp.concatenate([vn[c * CHUNK:(c + 1) * CHUNK] for c in range(n_chunks)],
                              axis=1).astype(BF16)
        mixed_cat = _dot(ws, rhs)
        mixed = jnp.concatenate(
            [mixed_cat[:, c * gd:(c + 1) * gd] + bs_ref[grp] for c in range(n_chunks)], axis=0)
        gated.append((z[:, lanes] * mixed).astype(BF16))
    a = jnp.concatenate(gated, axis=1)
    o_ref[...] = x + _dot(a, wout_ref[...])


def _gmlp_prompt(x, g, w_in, vn_g, vn_b, w_s, b_s, w_out, *, batch, tm):
    t, d = x.shape
    width = w_out.shape[0]
    gd = width // GMLP_GROUPS
    tiles = t // batch // tm
    bs_b = jnp.broadcast_to(b_s[:, :, None], (GMLP_GROUPS, CHUNK, gd))
    const2 = lambda b, i: (0, 0)
    const3 = lambda b, i: (0, 0, 0)
    return pl.pallas_call(
        _gmlp_prompt_kernel,
        grid=(batch, tiles),
        in_specs=[
            pl.BlockSpec((tm, d), lambda b, i: (b * tiles + i, 0)),
            pl.BlockSpec((1, d), const2),
            pl.BlockSpec((d, 2 * width), const2),
            pl.BlockSpec((1, width), const2),
            pl.BlockSpec((1, width), const2),
            pl.BlockSpec((GMLP_GROUPS, CHUNK, CHUNK), const3),
            pl.BlockSpec((GMLP_GROUPS, CHUNK, gd), const3),
            pl.BlockSpec((width, d), const2),
        ],
        out_specs=[
            pl.BlockSpec((tm, d), lambda b, i: (b * tiles + i, 0)),
            pl.BlockSpec((1, CHUNK, width), lambda b, i: (b, 0, 0)),
        ],
        out_shape=[jax.ShapeDtypeStruct((t, d), F32),
                   jax.ShapeDtypeStruct((batch, CHUNK, width), F32)],
        compiler_params=_params(2),
        name="gmlp_prompt",
    )(x, g.reshape(1, d), w_in, vn_g.reshape(1, width), vn_b.reshape(1, width), w_s, bs_b, w_out)


def _gmlp_sample_kernel(x_ref, g_ref, win_ref, vg_ref, vb_ref, coef_ref, bias_ref, wout_ref,
                        o_ref, sv_ref):
    width = win_ref.shape[1] // 2
    gd = width // GMLP_GROUPS
    x = x_ref[...]
    h = _rms(x, g_ref[...]).astype(BF16)
    z = _gelu_tanh(_dot(h, win_ref[...]))
    vn = jnp.concatenate(
        [_group_layernorm(z[:, width + grp * gd: width + (grp + 1) * gd],
                          vg_ref[:, grp * gd:(grp + 1) * gd], vb_ref[:, grp * gd:(grp + 1) * gd])
         for grp in range(GMLP_GROUPS)], axis=1)
    sv_ref[...] = vn
    mixed = vn * coef_ref[...] + bias_ref[...]
    o_ref[...] = x + _dot((z[:, :width] * mixed).astype(BF16), wout_ref[...])


def _gmlp_sample(x, g, w_in, vn_g, vn_b, w_s, b_s, w_out):
    t, d = x.shape
    width = w_out.shape[0]
    gd = width // GMLP_GROUPS
    coef = jnp.repeat(w_s[:, 0, 0], gd).reshape(1, width)
    bias = jnp.repeat(b_s[:, 0], gd).reshape(1, width)
    return pl.pallas_call(
        _gmlp_sample_kernel,
        out_shape=[jax.ShapeDtypeStruct((t, d), F32), jax.ShapeDtypeStruct((t, width), F32)],
        compiler_params=pltpu.CompilerParams(vmem_limit_bytes=VMEM_LIMIT_BYTES),
        name="gmlp_sample",
    )(x, g.reshape(1, d), w_in, vn_g.reshape(1, width), vn_b.reshape(1, width), coef, bias, w_out)


LOG2_E = 1.4426950408889634
Q_SCALE_LOG2 = HEAD_DIM ** -0.5 * LOG2_E


def _qkv_prompt_kernel(x_ref, g_ref, wt_ref, wk_ref, ktf_ref, vtf_ref, qtb_ref, kab_ref, vtb_ref,
                       mean_ref):
    tm, d = x_ref.shape
    n_pairs = d // LANES
    h = _rms(x_ref[...], g_ref[...]).astype(BF16)
    qkv_t = _dot_nt(wt_ref[...], h)
    ktf_ref[0] = qkv_t[d:2 * d]
    vtf_ref[0] = qkv_t[2 * d:]
    qtb_ref[0, 0] = (qkv_t[:d] * Q_SCALE_LOG2).astype(BF16)
    vtb_ref[0, 0] = qkv_t[2 * d:].astype(BF16)
    k = _dot(h, wk_ref[...])
    mean_ref[0] = jnp.sum(k, axis=0, keepdims=True) * (1.0 / MOBA_BLOCK)
    kb = k.astype(BF16)
    lane = lax.broadcasted_iota(jnp.int32, (tm, LANES), 1)
    onehot = jnp.where(lane == pl.program_id(1), 1.0, 0.0).astype(BF16)
    pieces = []
    for hp in range(n_pairs):
        pieces += [kb[:, hp * LANES:(hp + 1) * LANES], onehot]
    kab_ref[...] = jnp.concatenate(pieces, axis=1)


def _qkv_prompt(x, g, w_t, w_k, *, batch):
    t, d = x.shape
    tm = MOBA_BLOCK
    nb = t // batch // tm
    assert nb <= LANES
    const = lambda b, i: (0, 0)
    feat_spec = pl.BlockSpec((1, d, tm), lambda b, i: (b, 0, i))
    blk_spec = pl.BlockSpec((1, 1, d, tm), lambda b, i: (b, i, 0, 0))
    return pl.pallas_call(
        _qkv_prompt_kernel,
        grid=(batch, nb),
        in_specs=[pl.BlockSpec((tm, d), lambda b, i: (b * nb + i, 0)), pl.BlockSpec((1, d), const),
                  pl.BlockSpec((3 * d, d), const), pl.BlockSpec((d, d), const)],
        out_specs=[feat_spec, feat_spec, blk_spec,
                   pl.BlockSpec((tm, 2 * d), lambda b, i: (b * nb + i, 0)), blk_spec,
                   pl.BlockSpec((1, 1, d), lambda b, i: (b * nb + i, 0, 0))],
        out_shape=[jax.ShapeDtypeStruct((batch, d, nb * tm), F32)] * 2
        + [jax.ShapeDtypeStruct((batch, nb, d, tm), BF16), jax.ShapeDtypeStruct((t, 2 * d), BF16),
           jax.ShapeDtypeStruct((batch, nb, d, tm), BF16), jax.ShapeDtypeStruct((batch * nb, 1, d), F32)],
        compiler_params=_params(2),
        name="qkv_prompt",
    )(x, g.reshape(1, d), w_t, w_k)


def _qkv_sample_kernel(x_ref, g_ref, w_ref, q_ref, k_ref, v_ref):
    d = x_ref.shape[1]
    h = _rms(x_ref[...], g_ref[...]).astype(BF16)
    qkv = _dot(h, w_ref[...])
    q_ref[...] = qkv[:, :d]
    k_ref[...] = qkv[:, d:2 * d]
    v_ref[...] = qkv[:, 2 * d:]


def _qkv_sample(x, g, w):
    t, d = x.shape
    return pl.pallas_call(
        _qkv_sample_kernel,
        out_shape=[jax.ShapeDtypeStruct((t, d), F32)] * 3,
        compiler_params=pltpu.CompilerParams(vmem_limit_bytes=VMEM_LIMIT_BYTES),
        name="qkv_sample",
    )(x, g.reshape(1, d), w)


def _top_k_mask(scores, n_iota, n_valid):
    axis = scores.ndim - 2
    n_total = scores.shape[axis]
    s = jnp.where(n_iota < n_valid, scores, NEG_INF)
    picked = jnp.zeros(scores.shape, F32)
    idxs = []
    for _ in range(MOBA_TOPK):
        best = jnp.max(s, axis=axis, keepdims=True)
        idx = jnp.min(jnp.where(s == best, n_iota, n_total), axis=axis, keepdims=True)
        hit = n_iota == idx
        picked = jnp.where(hit, 1.0, picked)
        s = jnp.where(hit, BELOW_NEG_INF, s)
        idxs.append(idx)
    return picked, idxs


def _select_prompt_kernel(qt_ref, mt_ref, o_ref):
    tq = qt_ref.shape[3]
    nb = mt_ref.shape[1] // N_HEADS
    own = pl.program_id(1)
    st = _dot(mt_ref[0], qt_ref[0, 0])
    st = st.reshape(N_HEADS, nb, tq)
    n_iota = lax.broadcasted_iota(jnp.int32, st.shape, 1)
    picked, _ = _top_k_mask(st, n_iota, own)
    bias = jnp.where(n_iota < own, jnp.where(picked > 0.0, 0.0, NEG_INF), NEG_INF)
    o_ref[0] = bias.astype(BF16)


def _select_prompt(qt, means_t):
    batch, nb, d, tq = qt.shape
    return pl.pallas_call(
        _select_prompt_kernel,
        grid=(batch, nb),
        in_specs=[pl.BlockSpec((1, 1, d, tq), lambda b, i: (b, i, 0, 0)),
                  pl.BlockSpec((1, N_HEADS * nb, d), lambda b, i: (b, 0, 0))],
        out_specs=pl.BlockSpec((1, N_HEADS, nb, tq), lambda b, i: (b, 0, 0, i)),
        out_shape=jax.ShapeDtypeStruct((batch, N_HEADS, nb, nb * tq), BF16),
        compiler_params=_params(2),
        name="select_prompt",
    )(qt, means_t)


PAIRS_PER_STEP = 2
BLOCKS_PER_TRIP = 4
SCORE_LEAD = 8


PAGES_PER_STEP = 16
SUM_ROWS = 16


def _accumulate_block_key_sums(page_refs, sums_ref, part):
    blocks_per_step = PAGES_PER_STEP // PAGES_PER_BLOCK

    @pl.when(part == 0)
    def _():
        sums_ref[...] = jnp.zeros_like(sums_ref)

    lane = lax.broadcasted_iota(jnp.int32, sums_ref.shape[1:], 2)
    acc = sums_ref[0]
    for blk in range(blocks_per_step):
        tot = page_refs[blk * PAGES_PER_BLOCK][0]
        for j in range(1, PAGES_PER_BLOCK):
            tot = tot + page_refs[blk * PAGES_PER_BLOCK + j][0]
        col = jnp.sum(tot, axis=-1, keepdims=True)
        acc = jnp.where(lane == part * blocks_per_step + blk, col, acc)
    sums_ref[0] = acc


def _attn_prompt_kernel(pt_ref, qt_ref, bias_ref, ka_ref, vt_ref, *rest, steps_per_seq):
    del pt_ref
    page_refs = rest[:PAGES_PER_STEP]
    o_ref, sums_ref, qa_ref = rest[PAGES_PER_STEP:]
    _accumulate_block_key_sums(page_refs, sums_ref, _rem(pl.program_id(2), steps_per_seq))

    tq = qt_ref.shape[3]
    nb = bias_ref.shape[2]
    qi = pl.program_id(2)
    n_chains = PAIRS_PER_STEP * HEADS_PER_TILE
    ones_rows = jnp.ones((SUM_ROWS, MOBA_BLOCK), BF16)
    feat = lax.broadcasted_iota(jnp.int32, (LANES, tq), 0)
    zeros_bias_pad = jnp.zeros((LANES - nb, tq), BF16)
    for c in range(n_chains):
        p, hh = divmod(c, HEADS_PER_TILE)
        qt = qt_ref[0, 0, p * LANES:(p + 1) * LANES, :]
        in_head = (feat >= hh * HEAD_DIM) & (feat < (hh + 1) * HEAD_DIM)
        qa_ref[c, :LANES, :] = jnp.where(in_head, qt, jnp.zeros_like(qt))
        qa_ref[c, LANES:, :] = jnp.concatenate([bias_ref[0, c], zeros_bias_pad], axis=0)

    def v_rows(c, blk):
        p, hh = divmod(c, HEADS_PER_TILE)
        r0 = p * LANES + hh * HEAD_DIM
        return jnp.concatenate([vt_ref[0, blk, r0:r0 + HEAD_DIM, :], ones_rows], axis=0)


    start = pl.multiple_of(qi * tq, tq)
    key_pos = lax.broadcasted_iota(jnp.int32, (tq, tq), 0)
    qry_pos = lax.broadcasted_iota(jnp.int32, (tq, tq), 1)
    scores = []
    for c in range(n_chains):
        p = c // HEADS_PER_TILE
        kd = ka_ref[pl.ds(start, tq), p * 2 * LANES:p * 2 * LANES + LANES]
        scores.append(_dot(kd, qa_ref[c, :LANES, :]))
    maxes, probs = [], []
    for c in range(n_chains):
        s = jnp.where(key_pos <= qry_pos, scores[c], NEG_INF).astype(BF16)
        m = jnp.max(s, axis=0, keepdims=True)
        maxes.append(m)
        probs.append(jnp.exp2(s - m))
    state = [x for c in range(n_chains) for x in (maxes[c], _dot(v_rows(c, qi), probs[c]))]

    def trip(first_block, n_blocks, state):
        blocks = [first_block + j for j in range(n_blocks)]
        lead = min(SCORE_LEAD, n_blocks * n_chains // 2)

        def score_product(c, n):
            p = c // HEADS_PER_TILE
            off = pl.multiple_of(n * MOBA_BLOCK, MOBA_BLOCK)
            ka = ka_ref[pl.ds(off, MOBA_BLOCK), p * 2 * LANES:(p + 1) * 2 * LANES]
            return _dot(ka, qa_ref[c]).astype(BF16)

        def online_softmax_update(m, acc, s, v_aug):
            m_new = jnp.maximum(m, jnp.max(s, axis=0, keepdims=True))
            acc = jnp.exp2((m - m_new).astype(F32)) * acc + _dot(v_aug, jnp.exp2(s - m_new))
            return m_new, acc

        units = [(c, n) for n in blocks for c in range(n_chains)]
        chain_state = {c: (state[2 * c], state[2 * c + 1]) for c in range(n_chains)}
        scores = {}
        for k in range(len(units) + lead):
            if k < len(units):
                scores[k] = score_product(*units[k])
            if k >= lead:
                c, n = units[k - lead]
                chain_state[c] = online_softmax_update(*chain_state[c], scores.pop(k - lead),
                                                       v_rows(c, n))
        return tuple(x for c in range(n_chains) for x in chain_state[c])

    n_trips = _div(qi + BLOCKS_PER_TRIP - 1, BLOCKS_PER_TRIP)
    state = lax.fori_loop(0, n_trips, lambda t, st: trip(t * BLOCKS_PER_TRIP, BLOCKS_PER_TRIP, st),
                          tuple(state))
    outs = []
    for c in range(n_chains):
        acc = state[2 * c + 1]
        outs.append((acc[:HEAD_DIM] / acc[HEAD_DIM:HEAD_DIM + 1]).T)
    o_ref[...] = jnp.concatenate(outs, axis=1).astype(o_ref.dtype)


def _attn_prompt(qt, bias, ka, vt, cache_kt, page_table, layer_page0):
    batch, nb, d, tq = qt.shape
    s = nb * tq
    n_groups = d // (PAIRS_PER_STEP * LANES)
    gw = PAIRS_PER_STEP * LANES
    n_chains = PAIRS_PER_STEP * HEADS_PER_TILE
    bs, n_pages = page_table.shape
    _, h, dh, page = cache_kt.shape
    steps_per_seq = n_pages // PAGES_PER_STEP
    assert nb % BLOCKS_PER_TRIP == 0 and nb <= LANES
    assert n_pages % PAGES_PER_STEP == 0 and n_pages // PAGES_PER_BLOCK < LANES
    assert nb % steps_per_seq == 0 and batch * n_groups * nb == bs * steps_per_seq, \
        "cache pages must tile the attention grid"
    seqs_per_row = nb // steps_per_seq

    def seq_of(b, g, i):
        return (b * n_groups + g) * seqs_per_row + _div(i, steps_per_seq)

    def page_spec(j):
        def index_map(b, g, i, pt):
            page_no = _rem(i, steps_per_seq) * PAGES_PER_STEP + j
            return (layer_page0 + pt[seq_of(b, g, i), page_no], 0, 0, 0)
        return pl.BlockSpec((1, h, dh, page), index_map)

    return pl.pallas_call(
        functools.partial(_attn_prompt_kernel, steps_per_seq=steps_per_seq),
        grid_spec=pltpu.PrefetchScalarGridSpec(
            num_scalar_prefetch=1,
            grid=(batch, n_groups, nb),
            in_specs=[pl.BlockSpec((1, 1, gw, tq), lambda b, g, i, pt: (b, i, g, 0)),
                      pl.BlockSpec((1, n_chains, nb, tq), lambda b, g, i, pt: (b, g, 0, i)),
                      pl.BlockSpec((s, 2 * gw), lambda b, g, i, pt: (b, g)),
                      pl.BlockSpec((1, nb, gw, tq), lambda b, g, i, pt: (b, 0, g, 0))]
            + [page_spec(j) for j in range(PAGES_PER_STEP)],
            out_specs=[pl.BlockSpec((tq, gw), lambda b, g, i, pt: (b * nb + i, g)),
                       pl.BlockSpec((1, h, dh, LANES),
                                    lambda b, g, i, pt: (seq_of(b, g, i), 0, 0, 0))],
            scratch_shapes=[pltpu.VMEM((n_chains, 2 * LANES, tq), BF16)],
        ),
        out_shape=[jax.ShapeDtypeStruct((batch * s, d), BF16),
                   jax.ShapeDtypeStruct((bs, h, dh, LANES), F32)],
        compiler_params=_params(3),
        name="attn_prompt",
    )(page_table, qt, bias, ka, vt, *([cache_kt] * PAGES_PER_STEP))


def _as_column(row):
    d = row.shape[1]
    return jnp.concatenate(
        [jnp.broadcast_to(row[:, c * LANES:(c + 1) * LANES], (LANES, LANES)).T
         for c in range(d // LANES)], axis=0)


def _select_sample_kernel(q_ref, knew_ref, sums_ref, o_ref, *, nb):
    _, h, dh, _ = sums_ref.shape
    lane3 = lax.broadcasted_iota(jnp.int32, (h, dh, LANES), 2)
    q_col = _as_column(q_ref[0]).reshape(h, dh, LANES)
    sums = jnp.where(lane3 == nb, _as_column(knew_ref[0]).reshape(h, dh, LANES), sums_ref[0])
    scores = jnp.sum(sums * (1.0 / MOBA_BLOCK) * q_col, axis=1)
    n_iota = lax.broadcasted_iota(jnp.int32, scores.shape, 1)
    s = jnp.where(n_iota < nb, scores, NEG_INF)
    picks = jnp.zeros(scores.shape, jnp.int32)
    for r in range(MOBA_TOPK):
        best = jnp.max(s, axis=1, keepdims=True)
        idx = jnp.min(jnp.where(s == best, n_iota, LANES), axis=1, keepdims=True)
        s = jnp.where(n_iota == idx, BELOW_NEG_INF, s)
        picks = jnp.where(n_iota == r, idx, picks)
    o_ref[0] = picks


def _select_sample(q, k_new, sums, nb):
    b, h, dh, _ = sums.shape
    d = h * dh
    return pl.pallas_call(
        functools.partial(_select_sample_kernel, nb=nb),
        grid=(b,),
        in_specs=[pl.BlockSpec((1, 1, d), lambda i: (i, 0, 0)),
                  pl.BlockSpec((1, 1, d), lambda i: (i, 0, 0)),
                  pl.BlockSpec((1, h, dh, LANES), lambda i: (i, 0, 0, 0))],
        out_specs=pl.BlockSpec((1, h, LANES), lambda i: (i, 0, 0)),
        out_shape=jax.ShapeDtypeStruct((b, h, LANES), jnp.int32),
        compiler_params=_params(1),
        name="select_sample",
    )(q, k_new, sums)


SAMPLE_HEADS_PER_STEP = 4
N_SEL_PAGES = SAMPLE_HEADS_PER_STEP * MOBA_TOPK * PAGES_PER_BLOCK


def _attn_sample_kernel(pages_ref, q_ref, knew_ref, vnew_ref, *refs):
    del pages_ref
    k_refs, v_refs, o_ref = refs[:N_SEL_PAGES], refs[N_SEL_PAGES:2 * N_SEL_PAGES], refs[-1]
    rows = 8
    width = SAMPLE_HEADS_PER_STEP * HEAD_DIM
    q = jnp.broadcast_to(q_ref[0] * (HEAD_DIM ** -0.5), (rows, width))
    k_new = jnp.broadcast_to(knew_ref[0], (rows, width))
    v_new = jnp.broadcast_to(vnew_ref[0], (rows, width))
    per_head = MOBA_TOPK * PAGES_PER_BLOCK
    heads = range(SAMPLE_HEADS_PER_STEP)
    cols = [slice(hh * HEAD_DIM, (hh + 1) * HEAD_DIM) for hh in heads]
    logits = [[_dot(q[:, cols[hh]].astype(BF16), k_refs[hh * per_head + c][0, 0].astype(BF16))
               for c in range(per_head)] for hh in heads]
    probs, denoms, accs = [], [], []
    for hh in heads:
        own = jnp.sum(q[:, cols[hh]] * k_new[:, cols[hh]], axis=1, keepdims=True)
        m = own
        for lg in logits[hh]:
            m = jnp.maximum(m, jnp.max(lg, axis=1, keepdims=True))
        p_own = jnp.exp(own - m)
        ps = [jnp.exp(lg - m) for lg in logits[hh]]
        denom = p_own
        for p in ps:
            denom = denom + jnp.sum(p, axis=1, keepdims=True)
        probs.append([p.astype(BF16) for p in ps])
        denoms.append(denom)
        accs.append(p_own * v_new[:, cols[hh]])
    outs = []
    for hh in heads:
        acc = accs[hh]
        for c in range(per_head):
            acc = acc + _dot_nt(probs[hh][c], v_refs[hh * per_head + c][0, 0].astype(BF16))
        outs.append(acc / denoms[hh])
    o_ref[0] = jnp.concatenate(outs, axis=1)[:1]


def _attn_sample(q, k_new, v_new, cache_kt, cache_vt, sel_pages):
    b, _, d = q.shape
    _, _, dh, page = cache_kt.shape
    n_groups = N_HEADS // SAMPLE_HEADS_PER_STEP
    vec_spec = pl.BlockSpec((1, 1, SAMPLE_HEADS_PER_STEP * HEAD_DIM),
                            lambda bi, hg, pages: (bi, 0, hg))

    def page_spec(hh, r, j):
        def index_map(bi, hg, pages):
            head = hg * SAMPLE_HEADS_PER_STEP + hh
            slot = ((bi * N_HEADS + head) * MOBA_TOPK + r) * PAGES_PER_BLOCK + j
            return (pages[slot], head, 0, 0)
        return pl.BlockSpec((1, 1, dh, page), index_map)

    page_specs = [page_spec(hh, r, j) for hh in range(SAMPLE_HEADS_PER_STEP)
                  for r in range(MOBA_TOPK) for j in range(PAGES_PER_BLOCK)]
    return pl.pallas_call(
        _attn_sample_kernel,
        grid_spec=pltpu.PrefetchScalarGridSpec(
            num_scalar_prefetch=1,
            grid=(b, n_groups),
            in_specs=[vec_spec] * 3 + page_specs * 2,
            out_specs=vec_spec,
        ),
        out_shape=jax.ShapeDtypeStruct((b, 1, d), F32),
        compiler_params=_params(2),
        name="attn_sample",
    )(sel_pages, q, k_new, v_new,
      *([cache_kt] * N_SEL_PAGES), *([cache_vt] * N_SEL_PAGES))


FFN_TILE = 1024
GMLP_TILE = 512


def kernel(x_prompt, x_sample, cache_k, cache_v, page_table, norm_ffn1, ffn1_gate, ffn1_up, ffn1_down, norm_mix, norm_ffn2, ffn2_gate, ffn2_up, ffn2_down, gmlp_in, gmlp_vnorm_g, gmlp_vnorm_b, gmlp_ws, gmlp_bs, gmlp_out, moba_qkv, moba_out, final_norm):
    bp, s, d = x_prompt.shape
    bs, t_new, _ = x_sample.shape
    assert t_new == 1 and s % MOBA_BLOCK == 0 and d == N_HEADS * HEAD_DIM
    depth = norm_ffn1.shape[0]
    n_pool, page = cache_k.shape[1], cache_k.shape[2]
    assert page == PAGE_SIZE
    nb_prompt = s // MOBA_BLOCK

    yp = x_prompt.reshape(bp * s, d)
    ys = x_sample.reshape(bs * t_new, d)
    cache_kt = jnp.transpose(cache_k, (0, 1, 3, 4, 2)).reshape(-1, N_HEADS, HEAD_DIM, PAGE_SIZE)
    cache_vt = jnp.transpose(cache_v, (0, 1, 3, 4, 2)).reshape(-1, N_HEADS, HEAD_DIM, PAGE_SIZE)
    head_of_col = jnp.arange(d) // HEAD_DIM

    sv_p, sv_s, kp_l, vp_l, ks_l, vs_l = [], [], [], [], [], []
    for i in range(depth):
        last = i == depth - 1
        yp = _ffn(yp, norm_ffn1[i], ffn1_gate, ffn1_up, ffn1_down, i, tm=FFN_TILE)
        ys = _ffn(ys, norm_ffn1[i], ffn1_gate, ffn1_up, ffn1_down, i, tm=bs)
        j = i // 2
        mix_p = mix_s = None
        if i % 2 == 0:
            w_in, w_out = gmlp_in[j].astype(BF16), gmlp_out[j].astype(BF16)
            yp, v_p = _gmlp_prompt(yp, norm_mix[i], w_in, gmlp_vnorm_g[j], gmlp_vnorm_b[j],
                                   gmlp_ws[j], gmlp_bs[j], w_out, batch=bp, tm=GMLP_TILE)
            ys, v_s = _gmlp_sample(ys, norm_mix[i], w_in, gmlp_vnorm_g[j], gmlp_vnorm_b[j],
                                   gmlp_ws[j], gmlp_bs[j], w_out)
            sv_p.append(v_p.reshape(bp, CHUNK, GMLP_GROUPS, -1))
            sv_s.append(v_s.reshape(bs, t_new, GMLP_GROUPS, -1))
        else:
            w_qkv, w_o = moba_qkv[j].astype(BF16), moba_out[j].astype(BF16)
            ktf, vtf, qt, ka, vt, means = _qkv_prompt(yp, norm_mix[i], w_qkv.T, w_qkv[:, d:2 * d],
                                                      batch=bp)
            means = means.reshape(bp, nb_prompt, d)
            means_t = jnp.where(jnp.arange(N_HEADS)[None, :, None, None] == head_of_col[None, None, None, :],
                                means[:, None, :, :], 0.0).reshape(bp, N_HEADS * nb_prompt, d).astype(BF16)
            bias = _select_prompt(qt, means_t)
            attn_p, sums = _attn_prompt(qt, bias, ka, vt, cache_kt, page_table, j * n_pool)
            mix_p = (attn_p, w_o)
            kp_l.append(jnp.transpose(ktf.reshape(bp, N_HEADS, HEAD_DIM, s), (0, 3, 1, 2)))
            vp_l.append(jnp.transpose(vtf.reshape(bp, N_HEADS, HEAD_DIM, s), (0, 3, 1, 2)))
            qs, kn, vn = _qkv_sample(ys, norm_mix[i], w_qkv)
            qs3, kn3, vn3 = qs.reshape(bs, 1, d), kn.reshape(bs, 1, d), vn.reshape(bs, 1, d)
            picks = _select_sample(qs3, kn3, sums, page_table.shape[1] // PAGES_PER_BLOCK)
            sel_page_no = (picks[:, :, :MOBA_TOPK, None] * PAGES_PER_BLOCK
                           + jnp.arange(PAGES_PER_BLOCK, dtype=jnp.int32))
            sel_pages = j * n_pool + jnp.take_along_axis(
                page_table[:, None, :], sel_page_no.reshape(bs, 1, -1), axis=2)
            attn_s = _attn_sample(qs3, kn3, vn3, cache_kt, cache_vt, sel_pages.reshape(-1))
            mix_s = (attn_s.reshape(bs, d), w_o)
            ks_l.append(kn.reshape(bs, t_new, N_HEADS, HEAD_DIM))
            vs_l.append(vn.reshape(bs, t_new, N_HEADS, HEAD_DIM))
        fg = final_norm if last else None
        yp = _ffn(yp, norm_ffn2[i], ffn2_gate, ffn2_up, ffn2_down, i, fg, mix_p, tm=FFN_TILE)
        ys = _ffn(ys, norm_ffn2[i], ffn2_gate, ffn2_up, ffn2_down, i, fg, mix_s, tm=bs)
    return (yp.reshape(bp, s, d), ys.reshape(bs, t_new, d), jnp.stack(sv_p), jnp.stack(sv_s),
            jnp.stack(kp_l), jnp.stack(vp_l), jnp.stack(ks_l), jnp.stack(vs_l))
```
